```python
import math
import jax, jax.numpy as jnp
from jax import lax
import numpy as np

D_MODEL = 1024
BATCH = 4
SEQ = 8192
DEPTH = 1
DEC_BATCH = 1
DEC_SEQ = 16384
PAST_LEN = 128

MIX_WIDTH = D_MODEL
GLA_WIDTH = MIX_WIDTH // 2
DIFF_WIDTH = MIX_WIDTH - GLA_WIDTH
GLA_HEADS = 4
GLA_DV = GLA_WIDTH // GLA_HEADS
GLA_DK = GLA_DV // 2
GLA_RANK = 16
GLA_GATE_TAU = 16.0
GLA_CHUNK = 64
DIFF_HEADS = 4
DIFF_DV = DIFF_WIDTH // DIFF_HEADS
DIFF_DH = DIFF_DV // 2
ROT_DIM = DIFF_DH // 4
ROPE_THETA = 500000.0
Q_BLOCK = 128
N_EXPERTS = 16
CAPACITY_FACTOR = 2
EXPERT_FF = ((8 * D_MODEL // 3 + 127) // 128) * 128
ALPHA = (2 * DEPTH) ** 0.25
BETA = (8 * DEPTH) ** -0.25
EPS = 1e-5
IN_SIZES = (GLA_HEADS * GLA_DK, GLA_HEADS * GLA_DK, GLA_WIDTH, GLA_WIDTH, GLA_RANK, GLA_RANK,
            DIFF_WIDTH, DIFF_WIDTH, DIFF_WIDTH)
IN_WIDTH = sum(IN_SIZES)

kernel_name = 'hymba_gla_diffattn_ec_moe_encoder'


def layer_norm(x, g, b):
    xf = x.astype(jnp.float32)
    mu = jnp.mean(xf, axis=-1, keepdims=True)
    var = jnp.mean(jnp.square(xf - mu), axis=-1, keepdims=True)
    return ((xf - mu) * lax.rsqrt(var + EPS) * g + b).astype(x.dtype)


def rms_norm(x, g):
    xf = x.astype(jnp.float32)
    return xf * lax.rsqrt(jnp.mean(jnp.square(xf), axis=-1, keepdims=True) + EPS) * g


def gla_chunked(q, k, v, log_a):
    B, S, H, dk = q.shape
    dv = v.shape[-1]
    n = S // GLA_CHUNK

    def to_chunks(t):
        return t.reshape(B, n, GLA_CHUNK, H, t.shape[-1]).transpose(0, 3, 1, 2, 4).astype(jnp.float32)

    q, k, v, log_a = (to_chunks(t) for t in (q, k, v, log_a))
    b = jnp.cumsum(log_a, axis=3)
    mid = b[:, :, :, GLA_CHUNK // 2 - 1:GLA_CHUNK // 2, :]
    scores = jnp.einsum('bhnid,bhnjd->bhnij', q * jnp.exp(b - mid), k * jnp.exp(mid - b))
    lower_tri = jnp.tril(jnp.ones((GLA_CHUNK, GLA_CHUNK), dtype=bool))
    scores = jnp.where(lower_tri, scores, 0.0)
    o_intra = jnp.einsum('bhnij,bhnjv->bhniv', scores, v)
    g_end = b[:, :, :, -1, :]
    upd = jnp.einsum('bhnjd,bhnjv->bhndv', k * jnp.exp(g_end[:, :, :, None, :] - b), v)

    def step(state, inp):
        decay, u = inp
        return decay[..., None] * state + u, state

    _, s_prev = lax.scan(step, jnp.zeros((B, H, dk, dv), jnp.float32),
                         (jnp.moveaxis(jnp.exp(g_end), 2, 0), jnp.moveaxis(upd, 2, 0)))
    s_prev = jnp.moveaxis(s_prev, 0, 2)
    o_inter = jnp.einsum('bhnid,bhndv->bhniv', q * jnp.exp(b), s_prev)
    return (o_intra + o_inter).transpose(0, 2, 3, 1, 4).reshape(B, S, H, dv)


def partial_rope(t, pos):
    inv_freq = ROPE_THETA ** (-jnp.arange(0, ROT_DIM, 2, dtype=jnp.float32) / ROT_DIM)
    ang = pos[:, None] * inv_freq[None, :]
    cos = jnp.cos(ang)[None, :, None, None, :]
    sin = jnp.sin(ang)[None, :, None, None, :]
    half = ROT_DIM // 2
    t1 = t[..., :half].astype(jnp.float32)
    t2 = t[..., half:ROT_DIM].astype(jnp.float32)
    rot = jnp.concatenate([t1 * cos - t2 * sin, t1 * sin + t2 * cos], axis=-1).astype(t.dtype)
    return jnp.concatenate([rot, t[..., ROT_DIM:]], axis=-1)


def diff_attention(q, k, v, lam):
    B, S, H, _, dh = q.shape
    nb = S // Q_BLOCK
    q_blocks = q.reshape(B, nb, Q_BLOCK, H, 2, dh).transpose(1, 0, 3, 4, 2, 5)
    k_t = k.transpose(0, 2, 3, 1, 4)
    v_t = v.transpose(0, 2, 1, 3).astype(jnp.float32)
    scale = dh ** -0.5

    def one_block(qb):
        s = jnp.einsum('bhcqd,bhckd->bhcqk', qb, k_t).astype(jnp.float32) * scale
        p = jax.nn.softmax(s, axis=-1)
        w = p[:, :, 0] - lam * p[:, :, 1]
        return jnp.einsum('bhqk,bhkv->bhqv', w, v_t)

    o = lax.map(one_block, q_blocks)
    return o.transpose(1, 0, 3, 2, 4).reshape(B, S, H, -1)


def token_mixer(x, layer, w_in, w_lr_f, b_lr_f, w_lr_b, b_lr_b, gla_norm_g,
                lambda_q1, lambda_k1, lambda_q2, lambda_k2, diff_norm_g, w_o):
    B, S, _ = x.shape
    proj = jnp.einsum('bsd,de->bse', x, w_in)
    split_at = np.cumsum(IN_SIZES)[:-1].tolist()
    gq, gk, gv, gg, zf, zb, dq, dk, dv = jnp.split(proj, split_at, axis=-1)

    def heads(t, h):
        return t.reshape(B, S, h, -1)

    q = heads(gq, GLA_HEADS) * (GLA_DK ** -0.5)
    k = heads(gk, GLA_HEADS)
    v = heads(gv, GLA_HEADS)
    la_f = heads(jax.nn.log_sigmoid(jnp.einsum('bsr,rk->bsk', zf, w_lr_f).astype(jnp.float32) + b_lr_f) / GLA_GATE_TAU, GLA_HEADS)
    la_b = heads(jax.nn.log_sigmoid(jnp.einsum('bsr,rk->bsk', zb, w_lr_b).astype(jnp.float32) + b_lr_b) / GLA_GATE_TAU, GLA_HEADS)

    def flip(t):
        return jnp.flip(t, axis=1)

    o_f = gla_chunked(q, k, v, la_f)
    o_b = flip(gla_chunked(flip(q), flip(k), flip(v), flip(la_b)))
    o_gla = rms_norm(o_f + o_b, gla_norm_g) * jax.nn.silu(heads(gg, GLA_HEADS).astype(jnp.float32))

    pos = jnp.arange(S, dtype=jnp.float32)
    dqh = partial_rope(dq.reshape(B, S, DIFF_HEADS, 2, DIFF_DH), pos)
    dkh = partial_rope(dk.reshape(B, S, DIFF_HEADS, 2, DIFF_DH), pos)
    dvh = dv.reshape(B, S, DIFF_HEADS, DIFF_DV)
    lambda_init = 0.8 - 0.6 * math.exp(-0.3 * layer)
    lam = (jnp.exp(jnp.sum(lambda_q1.astype(jnp.float32) * lambda_k1))
           - jnp.exp(jnp.sum(lambda_q2.astype(jnp.float32) * lambda_k2)) + lambda_init)
    o_diff = rms_norm(diff_attention(dqh, dkh, dvh, lam), diff_norm_g) * (1.0 - lambda_init)

    mixed = jnp.concatenate([o_gla.reshape(B, S, GLA_WIDTH), o_diff.reshape(B, S, DIFF_WIDTH)], axis=-1).astype(x.dtype)
    return jnp.einsum('bse,ed->bsd', mixed, w_o)


def ec_moe(h, w_router, w_gate, w_up, w_down):
    B, S, D = h.shape
    T = B * S
    C = (CAPACITY_FACTOR * T) // N_EXPERTS
    xt = h.reshape(T, D)
    aff = jax.nn.softmax(jnp.einsum('td,de->te', xt, w_router).astype(jnp.float32), axis=-1)
    gate, idx = lax.top_k(aff.T, C)
    xe = xt[idx]
    hid = jax.nn.silu(jnp.einsum('ecd,edf->ecf', xe, w_gate)) * jnp.einsum('ecd,edf->ecf', xe, w_up)
    ye = jnp.einsum('ecf,efd->ecd', hid, w_down) * gate[..., None].astype(hid.dtype)
    out = jnp.zeros((T, D), ye.dtype).at[idx.reshape(-1)].add(ye.reshape(-1, D))
    return out.reshape(B, S, D).astype(h.dtype)


def encoder_layer(x, layer, w_in, w_lr_f, b_lr_f, w_lr_b, b_lr_b, gla_norm_g,
                  lambda_q1, lambda_k1, lambda_q2, lambda_k2, diff_norm_g, w_o,
                  ln1_g, ln1_b, w_router, w_gate, w_up, w_down, ln2_g, ln2_b):
    mix = token_mixer(x, layer, w_in, w_lr_f, b_lr_f, w_lr_b, b_lr_b, gla_norm_g,
                      lambda_q1, lambda_k1, lambda_q2, lambda_k2, diff_norm_g, w_o)
    h = layer_norm(ALPHA * x + mix, ln1_g, ln1_b)
    return layer_norm(ALPHA * h + ec_moe(h, w_router, w_gate, w_up, w_down), ln2_g, ln2_b)


def setup_inputs(seed: int = 0) -> dict:
    key = jax.random.key(seed)
    ks = jax.random.split(key, 22)
    L = DEPTH

    def nrm(k, shape, scale):
        return jax.random.normal(k, shape, jnp.float32) * scale

    return {
        'x_prompt': nrm(ks[0], (BATCH, SEQ, D_MODEL), 1.0),
        'x_sample': nrm(ks[1], (DEC_BATCH, DEC_SEQ, D_MODEL), 1.0),
        'w_in': nrm(ks[2], (L, D_MODEL, IN_WIDTH), D_MODEL ** -0.5),
        'w_lr_f': nrm(ks[3], (L, GLA_RANK, GLA_HEADS * GLA_DK), GLA_RANK ** -0.5),
        'b_lr_f': nrm(ks[4], (L, GLA_HEADS * GLA_DK), 0.1),
        'w_lr_b': nrm(ks[5], (L, GLA_RANK, GLA_HEADS * GLA_DK), GLA_RANK ** -0.5),
        'b_lr_b': nrm(ks[6], (L, GLA_HEADS * GLA_DK), 0.1),
        'gla_norm_g': 1.0 + nrm(ks[7], (L, GLA_DV), 0.02),
        'lambda_q1': nrm(ks[8], (L, DIFF_DH), 0.1),
        'lambda_k1': nrm(ks[9], (L, DIFF_DH), 0.1),
        'lambda_q2': nrm(ks[10], (L, DIFF_DH), 0.1),
        'lambda_k2': nrm(ks[11], (L, DIFF_DH), 0.1),
        'diff_norm_g': 1.0 + nrm(ks[12], (L, DIFF_DV), 0.02),
        'w_o': nrm(ks[13], (L, MIX_WIDTH, D_MODEL), BETA * MIX_WIDTH ** -0.5),
        'ln1_g': 1.0 + nrm(ks[14], (L, D_MODEL), 0.02),
        'ln1_b': nrm(ks[15], (L, D_MODEL), 0.02),
        'w_router': nrm(ks[16], (L, D_MODEL, N_EXPERTS), D_MODEL ** -0.5),
        'w_gate': nrm(ks[17], (L, N_EXPERTS, D_MODEL, EXPERT_FF), D_MODEL ** -0.5),
        'w_up': nrm(ks[18], (L, N_EXPERTS, D_MODEL, EXPERT_FF), D_MODEL ** -0.5),
        'w_down': nrm(ks[19], (L, N_EXPERTS, EXPERT_FF, D_MODEL), BETA * EXPERT_FF ** -0.5),
        'ln2_g': 1.0 + nrm(ks[20], (L, D_MODEL), 0.02),
        'ln2_b': nrm(ks[21], (L, D_MODEL), 0.02),
    }


def reference(x_prompt, x_sample, w_in, w_lr_f, b_lr_f, w_lr_b, b_lr_b, gla_norm_g,
              lambda_q1, lambda_k1, lambda_q2, lambda_k2, diff_norm_g, w_o,
              ln1_g, ln1_b, w_router, w_gate, w_up, w_down, ln2_g, ln2_b):
    weights = (w_in, w_lr_f, b_lr_f, w_lr_b, b_lr_b, gla_norm_g,
               lambda_q1, lambda_k1, lambda_q2, lambda_k2, diff_norm_g, w_o,
               ln1_g, ln1_b, w_router, w_gate, w_up, w_down, ln2_g, ln2_b)
    y_prompt = x_prompt
    y_sample = x_sample
    for layer in range(DEPTH):
        lw = [w[layer] for w in weights]
        y_prompt = encoder_layer(y_prompt, layer, *lw)
        y_sample = encoder_layer(y_sample, layer, *lw)
    return (y_prompt, y_sample)
```

```python
import functools
import math

import jax
import jax.numpy as jnp
from jax import lax
from jax.experimental import pallas as pl
from jax.experimental.pallas import tpu as pltpu

F32 = jnp.float32
BF16 = jnp.bfloat16
I32 = jnp.int32

D_MODEL = 1024
GLA_HEADS = 4
GLA_DK = 64
GLA_DV = 128
GLA_RANK = 16
GLA_GATE_TAU = 16.0
GLA_CHUNK = 64
GLA_QK = GLA_HEADS * GLA_DK
GLA_WIDTH = GLA_HEADS * GLA_DV
DIFF_HEADS = 4
DIFF_DV = 128
DIFF_DH = 64
DIFF_WIDTH = DIFF_HEADS * DIFF_DV
ROT_DIM = DIFF_DH // 4
ROPE_THETA = 500000.0
N_EXPERTS = 16
CAPACITY_FACTOR = 2
EXPERT_FF = 2816
DEPTH = 1
ALPHA = (2 * DEPTH) ** 0.25
EPS = 1e-5
LAMBDA_INIT = 0.8 - 0.6 * math.exp(-0.3 * 0)

LANES = 128
BF16_SUBLANES = 16
VMEM_LIMIT = 56 * 1024 * 1024

ROW_TILE = 512
RANK_SUB = 512
GLA_BLOCK = 256
SLOT_CHUNK = 256
FFN_TILE = 512
FFN_FCHUNK = 256

C_QKV = 0
C_GG = 1024
C_Z = 1536
C_DQK = 1664
C_DV = 2688
W_IN_COLS = 3200


def _nt_dot(a, b):
    return lax.dot_general(a, b, (((1,), (1,)), ((), ())), preferred_element_type=F32)


def _tn_dot(a, b):
    return lax.dot_general(a, b, (((0,), (0,)), ((), ())), preferred_element_type=F32)


def _dot(a, b):
    return jnp.dot(a, b, preferred_element_type=F32)


def _split3(x):
    hi = x.astype(BF16)
    r1 = x - hi.astype(F32)
    md = r1.astype(BF16)
    lo = (r1 - md.astype(F32)).astype(BF16)
    return hi, md, lo


def _layer_norm(v, g, b):
    mu = jnp.mean(v, axis=-1, keepdims=True)
    c = v - mu
    var = jnp.mean(c * c, axis=-1, keepdims=True)
    return c * lax.rsqrt(var + EPS) * g + b


def _silu(v):
    return v / (1.0 + jnp.exp(-v))


def _inproj_kernel(x_ref, w_ref, wlr_ref, blr_ref, ra_ref, rb_ref, rc_ref,
                   qkv_ref, gg_ref, la_ref, dq_ref, dk_ref, dv_ref):
    xb = x_ref[...].astype(BF16)

    def mm(lo, width):
        return _dot(xb, w_ref[:, lo:lo + width])

    g = mm(C_QKV, 1024)
    qkv_ref[:, 0:GLA_QK] = g[:, 0:GLA_QK] * (GLA_DK ** -0.5)
    qkv_ref[:, GLA_QK:1024] = g[:, GLA_QK:1024]
    gg_ref[...] = mm(C_GG, GLA_WIDTH)

    z = mm(C_Z, LANES).astype(BF16)
    pre = _dot(z, wlr_ref[...]) + blr_ref[...]
    log_sig = jnp.minimum(pre, 0.0) - jnp.log(1.0 + jnp.exp(-jnp.abs(pre)))
    la_ref[...] = log_sig * (1.0 / GLA_GATE_TAU)

    d = mm(C_DQK, 2 * DIFF_WIDTH)
    ra = ra_ref[...]
    rb = rb_ref[...]
    rc = rc_ref[...]
    for hh in range(2 * DIFF_HEADS):
        t = d[:, hh * LANES:(hh + 1) * LANES]
        y = t * ra + pltpu.roll(t, ROT_DIM // 2, 1) * rb + pltpu.roll(t, LANES - ROT_DIM // 2, 1) * rc
        if hh < DIFF_HEADS:
            dq_ref[:, hh * LANES:(hh + 1) * LANES] = (y * (DIFF_DH ** -0.5)).astype(BF16)
        else:
            h2 = hh - DIFF_HEADS
            dk_ref[:, h2 * LANES:(h2 + 1) * LANES] = y.astype(BF16)
    dv_ref[...] = mm(C_DV, DIFF_WIDTH).astype(BF16)


def _inproj(x2, w_all, wlr, blr, ra, rb, rc, seq_len):
    T = x2.shape[0]
    tm = ROW_TILE
    sblocks = seq_len // tm
    row = lambda i: (i, 0)
    const = lambda i: (0, 0)
    rope = lambda i: (i % sblocks, 0)
    return pl.pallas_call(
        _inproj_kernel,
        name="inproj",
        grid=(T // tm,),
        in_specs=[
            pl.BlockSpec((tm, D_MODEL), row),
            pl.BlockSpec((D_MODEL, W_IN_COLS), const),
            pl.BlockSpec((LANES, 2 * GLA_QK), const),
            pl.BlockSpec((1, 2 * GLA_QK), const),
            pl.BlockSpec((tm, LANES), rope),
            pl.BlockSpec((tm, LANES), rope),
            pl.BlockSpec((tm, LANES), rope),
        ],
        out_specs=[
            pl.BlockSpec((tm, 1024), row),
            pl.BlockSpec((tm, GLA_WIDTH), row),
            pl.BlockSpec((tm, 2 * GLA_QK), row),
            pl.BlockSpec((tm, DIFF_WIDTH), row),
            pl.BlockSpec((tm, DIFF_WIDTH), row),
            pl.BlockSpec((tm, DIFF_WIDTH), row),
        ],
        out_shape=[
            jax.ShapeDtypeStruct((T, 1024), F32),
            jax.ShapeDtypeStruct((T, GLA_WIDTH), F32),
            jax.ShapeDtypeStruct((T, 2 * GLA_QK), F32),
            jax.ShapeDtypeStruct((T, DIFF_WIDTH), BF16),
            jax.ShapeDtypeStruct((T, DIFF_WIDTH), BF16),
            jax.ShapeDtypeStruct((T, DIFF_WIDTH), BF16),
        ],
        compiler_params=pltpu.CompilerParams(
            dimension_semantics=("parallel",), vmem_limit_bytes=VMEM_LIMIT),
    )(x2, w_all, wlr, blr, ra, rb, rc)


def _gla_kernel(qkvf_ref, laf_ref, qkvb_ref, lab_ref, of_ref, ob_ref, sf_ref, sb_ref, *, nsub):
    @pl.when(pl.program_id(1) == 0)
    def _():
        sf_ref[...] = jnp.zeros(sf_ref.shape, F32)
        sb_ref[...] = jnp.zeros(sb_ref.shape, F32)

    L = GLA_CHUNK
    row = lax.broadcasted_iota(I32, (L, L), 0)
    col = lax.broadcasted_iota(I32, (L, L), 1)
    lower = row >= col
    upper = row <= col
    lane = lax.broadcasted_iota(I32, (1, GLA_QK), 1)
    hmask = [(lane >= h * GLA_DK) & (lane < (h + 1) * GLA_DK) for h in range(GLA_HEADS)]

    def chunk(qkv_ref, la_ref, o_ref, s_ref, c, keep, mid_row, end_row):
        r0 = c * L
        q = qkv_ref[r0:r0 + L, 0:GLA_QK]
        k = qkv_ref[r0:r0 + L, GLA_QK:2 * GLA_QK]
        vb = qkv_ref[r0:r0 + L, 2 * GLA_QK:1024].astype(BF16)
        la = la_ref[r0:r0 + L, :]
        tri = jnp.where(keep, 1.0, 0.0).astype(BF16)
        hi, md, lo = _split3(la)
        b = _dot(tri, hi) + _dot(tri, md) + _dot(tri, lo)
        bm = b[mid_row:mid_row + 1, :]
        be = b[end_row:end_row + 1, :]
        qs = q * jnp.exp(b - bm)
        ks = (k * jnp.exp(bm - b)).astype(BF16)
        kd = (k * jnp.exp(be - b)).astype(BF16)
        qe = q * jnp.exp(b)
        dec = jnp.exp(be)
        qstack = jnp.concatenate([jnp.where(hmask[h], qs, 0.0) for h in range(GLA_HEADS)],
                                 axis=0).astype(BF16)
        sc = _nt_dot(qstack, ks)
        for h in range(GLA_HEADS):
            s_h = jnp.where(keep, sc[h * L:(h + 1) * L, :], 0.0).astype(BF16)
            v_h = vb[:, h * GLA_DV:(h + 1) * GLA_DV]
            st = s_ref[h]
            qe_h = jnp.where(hmask[h], qe, 0.0).astype(BF16)
            o_ref[r0:r0 + L, h * GLA_DV:(h + 1) * GLA_DV] = (
                _dot(s_h, v_h) + _nt_dot(qe_h, st.astype(BF16)))
            s_ref[h] = st * dec + _tn_dot(v_h, kd)

    for c in range(nsub):
        chunk(qkvf_ref, laf_ref, of_ref, sf_ref, c, lower, L // 2 - 1, L - 1)
    for c in range(nsub - 1, -1, -1):
        chunk(qkvb_ref, lab_ref, ob_ref, sb_ref, c, upper, L // 2, 0)


def _gla(qkv, la):
    B, S, _ = qkv.shape
    gb = GLA_BLOCK
    n = S // gb
    fwd = lambda b, i: (b, i, 0)
    bwd = lambda b, i: (b, n - 1 - i, 0)
    bwd_la = lambda b, i: (b, n - 1 - i, 1)
    return pl.pallas_call(
        functools.partial(_gla_kernel, nsub=gb // GLA_CHUNK),
        name="gla",
        grid=(B, n),
        in_specs=[
            pl.BlockSpec((None, gb, 1024), fwd),
            pl.BlockSpec((None, gb, GLA_QK), fwd),
            pl.BlockSpec((None, gb, 1024), bwd),
            pl.BlockSpec((None, gb, GLA_QK), bwd_la),
        ],
        out_specs=[
            pl.BlockSpec((None, gb, GLA_WIDTH), fwd),
            pl.BlockSpec((None, gb, GLA_WIDTH), bwd),
        ],
        out_shape=[jax.ShapeDtypeStruct((B, S, GLA_WIDTH), F32)] * 2,
        scratch_shapes=[pltpu.VMEM((GLA_HEADS, GLA_DV, GLA_QK), F32)] * 2,
        compiler_params=pltpu.CompilerParams(
            dimension_semantics=("parallel", "arbitrary"), vmem_limit_bytes=VMEM_LIMIT),
    )(qkv, la, qkv, la)


def _attn_kernel(lam_ref, g_ref, q_ref, k_ref, v_ref, o_ref,
                 q1_ref, q2_ref, m1_ref, l1_ref, a1_ref, m2_ref, l2_ref, a2_ref):
    ki = pl.program_id(3)

    @pl.when(ki == 0)
    def _():
        q = q_ref[...]
        lane = lax.broadcasted_iota(I32, (1, LANES), 1)
        zero = jnp.zeros_like(q)
        q1_ref[...] = jnp.where(lane < DIFF_DH, q, zero)
        q2_ref[...] = jnp.where(lane >= DIFF_DH, q, zero)
        for m_ref, l_ref, a_ref in ((m1_ref, l1_ref, a1_ref), (m2_ref, l2_ref, a2_ref)):
            m_ref[...] = jnp.full(m_ref.shape, -jnp.inf, F32)
            l_ref[...] = jnp.zeros(l_ref.shape, F32)
            a_ref[...] = jnp.zeros(a_ref.shape, F32)

    k = k_ref[...]
    v = v_ref[...]

    def update(qm_ref, m_ref, l_ref, a_ref):
        s = _nt_dot(qm_ref[...], k)
        m_old = m_ref[...]
        m_new = jnp.maximum(m_old, jnp.max(s, axis=1, keepdims=True))
        corr = jnp.exp(m_old - m_new)
        p = jnp.exp(s - m_new)
        l_ref[...] = corr * l_ref[...] + jnp.sum(p, axis=1, keepdims=True)
        a_ref[...] = corr * a_ref[...] + _dot(p.astype(BF16), v)
        m_ref[...] = m_new

    update(q1_ref, m1_ref, l1_ref, a1_ref)
    update(q2_ref, m2_ref, l2_ref, a2_ref)

    @pl.when(ki == pl.num_programs(3) - 1)
    def _():
        lp = lam_ref[...]
        lam = (jnp.exp(jnp.sum(lp[0:1] * lp[1:2], axis=1, keepdims=True))
               - jnp.exp(jnp.sum(lp[2:3] * lp[3:4], axis=1, keepdims=True)) + LAMBDA_INIT)
        o = a1_ref[...] / l1_ref[...] - lam * (a2_ref[...] / l2_ref[...])
        r = o * lax.rsqrt(jnp.mean(o * o, axis=1, keepdims=True) + EPS) * g_ref[...]
        o_ref[...] = (r * (1.0 - LAMBDA_INIT)).astype(BF16)


def _attn(lam4, g, dq, dk, dv, qb, kb):
    B, S, _ = dq.shape
    qmap = lambda b, h, qi, ki: (b, qi, h)
    kmap = lambda b, h, qi, ki: (b, ki, h)
    const = lambda b, h, qi, ki: (0, 0)
    return pl.pallas_call(
        _attn_kernel,
        name="attn",
        grid=(B, DIFF_HEADS, S // qb, S // kb),
        in_specs=[
            pl.BlockSpec((4, DIFF_DH), const),
            pl.BlockSpec((1, DIFF_DV), const),
            pl.BlockSpec((None, qb, LANES), qmap),
            pl.BlockSpec((None, kb, LANES), kmap),
            pl.BlockSpec((None, kb, LANES), kmap),
        ],
        out_specs=pl.BlockSpec((None, qb, LANES), qmap),
        out_shape=jax.ShapeDtypeStruct((B, S, DIFF_WIDTH), BF16),
        scratch_shapes=[
            pltpu.VMEM((qb, LANES), BF16), pltpu.VMEM((qb, LANES), BF16),
            pltpu.VMEM((qb, 1), F32), pltpu.VMEM((qb, 1), F32), pltpu.VMEM((qb, DIFF_DV), F32),
            pltpu.VMEM((qb, 1), F32), pltpu.VMEM((qb, 1), F32), pltpu.VMEM((qb, DIFF_DV), F32),
        ],
        compiler_params=pltpu.CompilerParams(
            dimension_semantics=("parallel", "parallel", "parallel", "arbitrary"),
            vmem_limit_bytes=VMEM_LIMIT),
    )(lam4, g, dq, dk, dv)


def _post_kernel(of_ref, ob_ref, gg_ref, md_ref, x_ref, gn_ref, wo_ref, g1_ref, b1_ref,
                 wr_ref, wrt_ref, h_ref, hb_ref, aff_ref, afft_ref):
    gn = gn_ref[...]
    parts = []
    for h in range(GLA_HEADS):
        sl = slice(h * GLA_DV, (h + 1) * GLA_DV)
        o = of_ref[:, sl] + ob_ref[:, sl]
        r = o * lax.rsqrt(jnp.mean(o * o, axis=1, keepdims=True) + EPS) * gn
        parts.append((r * _silu(gg_ref[:, sl])).astype(BF16))
    mixed_gla = jnp.concatenate(parts, axis=1)
    mix = _dot(mixed_gla, wo_ref[0:GLA_WIDTH, :]) + _dot(md_ref[...], wo_ref[GLA_WIDTH:, :])
    hval = _layer_norm(ALPHA * x_ref[...] + mix, g1_ref[...], b1_ref[...])
    h_ref[...] = hval
    hh = hval.astype(BF16)
    hb_ref[...] = hh
    hl = (hval - hh.astype(F32)).astype(BF16)
    wr = wr_ref[...]
    wh = wr.astype(BF16)
    wl = (wr - wh.astype(F32)).astype(BF16)
    logits = _dot(hh, wh) + _dot(hl, wh) + _dot(hh, wl)
    e = jnp.exp(logits - jnp.max(logits, axis=1, keepdims=True))
    aff_ref[...] = e / jnp.sum(e, axis=1, keepdims=True)
    wrt = wrt_ref[...]
    wth = wrt.astype(BF16)
    wtl = (wrt - wth.astype(F32)).astype(BF16)
    lt = _nt_dot(wth, hh) + _nt_dot(wth, hl) + _nt_dot(wtl, hh)
    et = jnp.exp(lt - jnp.max(lt, axis=0, keepdims=True))
    afft_ref[...] = et / jnp.sum(et, axis=0, keepdims=True)


def _post(o_f, o_b, gg, md, x2, gn, wo, g1, b1, wr, wrt):
    T = x2.shape[0]
    tm = ROW_TILE
    assert tm == RANK_SUB
    row = lambda i: (i, 0)
    const = lambda i: (0, 0)
    return pl.pallas_call(
        _post_kernel,
        name="post",
        grid=(T // tm,),
        in_specs=[
            pl.BlockSpec((tm, GLA_WIDTH), row),
            pl.BlockSpec((tm, GLA_WIDTH), row),
            pl.BlockSpec((tm, GLA_WIDTH), row),
            pl.BlockSpec((tm, DIFF_WIDTH), row),
            pl.BlockSpec((tm, D_MODEL), row),
            pl.BlockSpec((1, GLA_DV), const),
            pl.BlockSpec((D_MODEL, D_MODEL), const),
            pl.BlockSpec((1, D_MODEL), const),
            pl.BlockSpec((1, D_MODEL), const),
            pl.BlockSpec((D_MODEL, N_EXPERTS), const),
            pl.BlockSpec((N_EXPERTS, D_MODEL), const),
        ],
        out_specs=[
            pl.BlockSpec((tm, D_MODEL), row),
            pl.BlockSpec((tm, D_MODEL), row),
            pl.BlockSpec((tm, N_EXPERTS), row),
            pl.BlockSpec((None, N_EXPERTS, tm), lambda i: (i, 0, 0)),
        ],
        out_shape=[
            jax.ShapeDtypeStruct((T, D_MODEL), F32),
            jax.ShapeDtypeStruct((T, D_MODEL), BF16),
            jax.ShapeDtypeStruct((T, N_EXPERTS), F32),
            jax.ShapeDtypeStruct((T // tm, N_EXPERTS, tm), F32),
        ],
        compiler_params=pltpu.CompilerParams(
            dimension_semantics=("parallel",), vmem_limit_bytes=VMEM_LIMIT),
    )(o_f, o_b, gg, md, x2, gn, wo, g1, b1, wr, wrt)


def _route_kernel(aff_ref, lr_ref, cnt_ref, off_ref, tot_ref, *, cap, nblk, spb):
    E = N_EXPERTS

    def count(mask):
        part = jnp.sum(jnp.where(mask, 1, 0).astype(I32), axis=0)
        return jnp.sum(part, axis=1, keepdims=True)

    def search(i, thr):
        cand = thr | jnp.left_shift(jnp.int32(1), 30 - i)
        bits = pltpu.bitcast(aff_ref[...], I32)
        return jnp.where(count(bits >= cand[None]) >= cap, cand, thr)

    thr = lax.fori_loop(0, 31, search, jnp.zeros((E, 1), I32))
    bits_all = pltpu.bitcast(aff_ref[...], I32)
    need = cap - count(bits_all > thr[None])

    r = lax.broadcasted_iota(I32, (RANK_SUB, RANK_SUB), 0)
    c = lax.broadcasted_iota(I32, (RANK_SUB, RANK_SUB), 1)
    before = jnp.where(r < c, 1.0, 0.0).astype(BF16)

    def block(j, carry):
        eqc, offc = carry
        selc = jnp.zeros((E, 1), I32)
        for s in range(spb):
            idx = j * spb + s
            bits = pltpu.bitcast(aff_ref[idx], I32)
            gt = bits > thr
            eq = bits == thr
            eqrank = _dot(jnp.where(eq, 1.0, 0.0).astype(BF16), before).astype(I32) + eqc
            sel = gt | (eq & (eqrank < need))
            rank = _dot(jnp.where(sel, 1.0, 0.0).astype(BF16), before).astype(I32) + selc
            lr_ref[idx] = jnp.where(sel, rank, -1)
            eqc = eqc + jnp.sum(jnp.where(eq, 1, 0).astype(I32), axis=1, keepdims=True)
            selc = selc + jnp.sum(jnp.where(sel, 1, 0).astype(I32), axis=1, keepdims=True)
        cnt_ref[j] = jnp.broadcast_to(selc, (E, LANES))
        off_ref[j] = jnp.broadcast_to(offc, (E, LANES))
        padded = (selc + (BF16_SUBLANES - 1)) & (-BF16_SUBLANES)
        return eqc, offc + padded

    zero = jnp.zeros((E, 1), I32)
    _, total = lax.fori_loop(0, nblk, block, (zero, zero))
    tot_ref[...] = jnp.broadcast_to(total, (E, LANES))


def _route(aff_blocks, cap, tb):
    nsb = aff_blocks.shape[0]
    spb = tb // RANK_SUB
    nblk = nsb // spb
    E = N_EXPERTS
    return pl.pallas_call(
        functools.partial(_route_kernel, cap=cap, nblk=nblk, spb=spb),
        name="route",
        out_shape=[
            jax.ShapeDtypeStruct((nsb, E, RANK_SUB), I32),
            jax.ShapeDtypeStruct((nblk, E, LANES), I32),
            jax.ShapeDtypeStruct((nblk, E, LANES), I32),
            jax.ShapeDtypeStruct((E, LANES), I32),
        ],
        compiler_params=pltpu.CompilerParams(vmem_limit_bytes=VMEM_LIMIT),
    )(aff_blocks)


def _gather_kernel(cnt_s, off_s, h_ref, lr_ref, xe_in, xe_out, xbuf, sem, *, spb):
    del xe_in
    j = pl.program_id(0)
    E = N_EXPERTS
    RC = SLOT_CHUNK

    def nchunks(e):
        return (cnt_s[j * E + e] + (RC - 1)) // RC

    def copy(e, slot, base):
        off = off_s[j * E + e]
        return pltpu.make_async_copy(
            xbuf.at[slot, pl.ds(base, RC)],
            xe_out.at[e, pl.ds(pl.multiple_of(off + base, BF16_SUBLANES), RC)],
            sem.at[slot])

    def drain(e, slot):
        def body(rc, carry):
            copy(e, slot, pl.multiple_of(rc * RC, RC)).wait()
            return carry
        lax.fori_loop(0, nchunks(e), body, 0)

    for e in range(E):
        slot = e % 2
        if e >= 2:
            drain(e - 2, slot)

        def body(rc, carry, e=e, slot=slot):
            base = pl.multiple_of(rc * RC, RC)
            rows = base + lax.broadcasted_iota(I32, (RC, 1), 0)
            acc = jnp.zeros((RC, D_MODEL), F32)
            for s in range(spb):
                lr = lr_ref[s, e:e + 1, :]
                onehot = jnp.where(lr == rows, 1.0, 0.0).astype(BF16)
                acc = acc + _dot(onehot, h_ref[s * RANK_SUB:(s + 1) * RANK_SUB, :])
            xbuf[slot, pl.ds(base, RC), :] = acc.astype(BF16)
            copy(e, slot, base).start()
            return carry
        lax.fori_loop(0, nchunks(e), body, 0)

    drain(E - 2, 0)
    drain(E - 1, 1)


def _gather(cnt_flat, off_flat, hb, lr, cp, tb):
    T = hb.shape[0]
    spb = tb // RANK_SUB
    E = N_EXPERTS
    xe0 = jnp.zeros((E, cp, D_MODEL), BF16)
    return pl.pallas_call(
        functools.partial(_gather_kernel, spb=spb),
        name="gather",
        grid_spec=pltpu.PrefetchScalarGridSpec(
            num_scalar_prefetch=2,
            grid=(T // tb,),
            in_specs=[
                pl.BlockSpec((tb, D_MODEL), lambda j, c, o: (j, 0)),
                pl.BlockSpec((spb, E, RANK_SUB), lambda j, c, o: (j, 0, 0)),
                pl.BlockSpec(memory_space=pl.ANY),
            ],
            out_specs=pl.BlockSpec(memory_space=pl.ANY),
            scratch_shapes=[pltpu.VMEM((2, tb, D_MODEL), BF16), pltpu.SemaphoreType.DMA((2,))],
        ),
        out_shape=jax.ShapeDtypeStruct((E, cp, D_MODEL), BF16),
        input_output_aliases={4: 0},
        compiler_params=pltpu.CompilerParams(
            dimension_semantics=("arbitrary",), vmem_limit_bytes=VMEM_LIMIT),
    )(cnt_flat, off_flat, hb, lr, xe0)


def _ffn_kernel(tot_s, x_ref, wg_ref, wu_ref, wd_ref, y_ref, acc_ref):
    e = pl.program_id(0)
    i = pl.program_id(1)
    tm = x_ref.shape[0]
    live = i * tm < tot_s[e]

    @pl.when(live)
    def _():
        x = x_ref[...]
        for c in range(EXPERT_FF // FFN_FCHUNK):
            cs = slice(c * FFN_FCHUNK, (c + 1) * FFN_FCHUNK)
            g = _dot(x, wg_ref[:, cs])
            u = _dot(x, wu_ref[:, cs])
            hid = (_silu(g) * u).astype(BF16)
            part = _dot(hid, wd_ref[cs, :])
            if c == 0:
                acc_ref[...] = part
            else:
                acc_ref[...] += part
        y_ref[...] = acc_ref[...].astype(BF16)

    @pl.when(jnp.logical_not(live))
    def _():
        y_ref[...] = jnp.zeros(y_ref.shape, BF16)


def _ffn(tot, xe, wg, wu, wd):
    E, cp, _ = xe.shape
    tm = FFN_TILE
    return pl.pallas_call(
        _ffn_kernel,
        name="ffn",
        grid_spec=pltpu.PrefetchScalarGridSpec(
            num_scalar_prefetch=1,
            grid=(E, cp // tm),
            in_specs=[
                pl.BlockSpec((None, tm, D_MODEL), lambda e, i, t: (e, i, 0)),
                pl.BlockSpec((None, D_MODEL, EXPERT_FF), lambda e, i, t: (e, 0, 0)),
                pl.BlockSpec((None, D_MODEL, EXPERT_FF), lambda e, i, t: (e, 0, 0)),
                pl.BlockSpec((None, EXPERT_FF, D_MODEL), lambda e, i, t: (e, 0, 0)),
            ],
            out_specs=pl.BlockSpec((None, tm, D_MODEL), lambda e, i, t: (e, i, 0)),
            scratch_shapes=[pltpu.VMEM((tm, D_MODEL), F32)],
        ),
        out_shape=jax.ShapeDtypeStruct((E, cp, D_MODEL), BF16),
        compiler_params=pltpu.CompilerParams(
            dimension_semantics=("parallel", "arbitrary"), vmem_limit_bytes=VMEM_LIMIT),
    )(tot, xe, wg, wu, wd)


def _combine_kernel(cnt_s, off_s, ye_hbm, lrt_ref, aff_ref, h_ref, g2_ref, b2_ref, y_ref,
                    ybuf, sem, *, spb):
    j = pl.program_id(0)
    E = N_EXPERTS
    RC = SLOT_CHUNK

    def nchunks(e):
        return (cnt_s[j * E + e] + (RC - 1)) // RC

    def copy(e, slot, base):
        off = off_s[j * E + e]
        return pltpu.make_async_copy(
            ye_hbm.at[e, pl.ds(pl.multiple_of(off + base, BF16_SUBLANES), RC)],
            ybuf.at[slot, pl.ds(base, RC)],
            sem.at[slot])

    def start(e, slot):
        def body(rc, carry):
            copy(e, slot, pl.multiple_of(rc * RC, RC)).start()
            return carry
        lax.fori_loop(0, nchunks(e), body, 0)

    def wait(e, slot):
        def body(rc, carry):
            copy(e, slot, pl.multiple_of(rc * RC, RC)).wait()
            return carry
        lax.fori_loop(0, nchunks(e), body, 0)

    start(0, 0)
    y_ref[...] = jnp.zeros(y_ref.shape, F32)
    for e in range(E):
        slot = e % 2
        if e + 1 < E:
            start(e + 1, 1 - slot)
        wait(e, slot)

        def body(rc, carry, e=e, slot=slot):
            base = pl.multiple_of(rc * RC, RC)
            ye = ybuf[slot, pl.ds(base, RC), :]
            cols = base + lax.broadcasted_iota(I32, (1, RC), 1)
            for s in range(spb):
                ts = slice(s * RANK_SUB, (s + 1) * RANK_SUB)
                lr = lrt_ref[ts, e:e + 1]
                onehot = jnp.where(lr == cols, 1.0, 0.0).astype(BF16)
                y_ref[ts, :] += _dot(onehot, ye) * aff_ref[ts, e:e + 1]
            return carry
        lax.fori_loop(0, nchunks(e), body, 0)

    y_ref[...] = _layer_norm(ALPHA * h_ref[...] + y_ref[...], g2_ref[...], b2_ref[...])


def _combine(cnt_flat, off_flat, ye, lrt, aff, h, g2, b2, tb):
    T = h.shape[0]
    spb = tb // RANK_SUB
    E = N_EXPERTS
    return pl.pallas_call(
        functools.partial(_combine_kernel, spb=spb),
        name="combine",
        grid_spec=pltpu.PrefetchScalarGridSpec(
            num_scalar_prefetch=2,
            grid=(T // tb,),
            in_specs=[
                pl.BlockSpec(memory_space=pl.ANY),
                pl.BlockSpec((tb, E), lambda j, c, o: (j, 0)),
                pl.BlockSpec((tb, E), lambda j, c, o: (j, 0)),
                pl.BlockSpec((tb, D_MODEL), lambda j, c, o: (j, 0)),
                pl.BlockSpec((1, D_MODEL), lambda j, c, o: (0, 0)),
                pl.BlockSpec((1, D_MODEL), lambda j, c, o: (0, 0)),
            ],
            out_specs=pl.BlockSpec((tb, D_MODEL), lambda j, c, o: (j, 0)),
            scratch_shapes=[pltpu.VMEM((2, tb, D_MODEL), BF16), pltpu.SemaphoreType.DMA((2,))],
        ),
        out_shape=jax.ShapeDtypeStruct((T, D_MODEL), F32),
        compiler_params=pltpu.CompilerParams(
            dimension_semantics=("arbitrary",), vmem_limit_bytes=VMEM_LIMIT),
    )(cnt_flat, off_flat, ye, lrt, aff, h, g2, b2)


def _rope_tables(seq_len):
    half = ROT_DIM // 2
    inv_freq = ROPE_THETA ** (-jnp.arange(0, ROT_DIM, 2, dtype=F32) / ROT_DIM)
    ang = jnp.arange(seq_len, dtype=F32)[:, None] * inv_freq[None, :]
    cos = jnp.cos(ang)
    sin = jnp.sin(ang)
    ones = jnp.ones((seq_len, DIFF_DH - ROT_DIM), F32)
    zeros = jnp.zeros((seq_len, DIFF_DH - ROT_DIM), F32)
    zh = jnp.zeros((seq_len, half), F32)
    ra = jnp.concatenate([cos, cos, ones], axis=1)
    rb = jnp.concatenate([zh, sin, zeros], axis=1)
    rc = jnp.concatenate([-sin, zh, zeros], axis=1)
    two = lambda t: jnp.concatenate([t, t], axis=1)
    return two(ra), two(rb), two(rc)


def _moe_tiles(T):
    tb = min(2048, T // 2)
    cap = (CAPACITY_FACTOR * T) // N_EXPERTS
    nblk = T // tb
    cp = cap + BF16_SUBLANES * nblk + SLOT_CHUNK
    cp = -(-cp // FFN_TILE) * FFN_TILE
    return tb, cap, cp


def _attn_tiles(S):
    return min(1024, S), min(512, S)


def _encoder_group(x, wts):
    B, S, D = x.shape
    T = B * S
    x2 = x.reshape(T, D)
    ra, rb, rc = _rope_tables(S)
    qkv, gg, la, dq, dk, dv = _inproj(x2, wts["w_all"], wts["wlr"], wts["blr"], ra, rb, rc, S)
    o_f, o_b = _gla(qkv.reshape(B, S, 1024), la.reshape(B, S, 2 * GLA_QK))
    qb, kb = _attn_tiles(S)
    md = _attn(wts["lam4"], wts["diff_g"], dq.reshape(B, S, DIFF_WIDTH),
               dk.reshape(B, S, DIFF_WIDTH), dv.reshape(B, S, DIFF_WIDTH), qb, kb)
    h, hb, aff, aff_blocks = _post(
        o_f.reshape(T, GLA_WIDTH), o_b.reshape(T, GLA_WIDTH), gg, md.reshape(T, DIFF_WIDTH), x2,
        wts["gla_g"], wts["w_o"], wts["ln1_g"], wts["ln1_b"], wts["w_router"], wts["w_router_t"])
    tb, cap, cp = _moe_tiles(T)
    lr, cnt, off, tot = _route(aff_blocks, cap, tb)
    cnt_flat = cnt[:, :, 0].reshape(-1)
    off_flat = off[:, :, 0].reshape(-1)
    lrt = lr.transpose(0, 2, 1).reshape(T, N_EXPERTS)
    xe = _gather(cnt_flat, off_flat, hb, lr, cp, tb)
    ye = _ffn(tot[:, 0], xe, wts["w_gate"], wts["w_up"], wts["w_down"])
    y = _combine(cnt_flat, off_flat, ye, lrt, aff, h, wts["ln2_g"], wts["ln2_b"], tb)
    return y.reshape(B, S, D)


def _prep_weights(w_in, w_lr_f, b_lr_f, w_lr_b, b_lr_b, gla_norm_g,
                  lambda_q1, lambda_k1, lambda_q2, lambda_k2, diff_norm_g, w_o,
                  ln1_g, ln1_b, w_router, w_gate, w_up, w_down, ln2_g, ln2_b):
    sizes = (GLA_QK, GLA_QK, GLA_WIDTH, GLA_WIDTH, GLA_RANK, GLA_RANK,
             DIFF_WIDTH, DIFF_WIDTH, DIFF_WIDTH)
    cols = []
    start = 0
    for s in sizes:
        cols.append(w_in[0][:, start:start + s])
        start += s
    gq, gk, gv, gg, zf, zb, dq, dk, dv = cols
    zpad = jnp.zeros((D_MODEL, LANES - 2 * GLA_RANK), F32)
    w_all = jnp.concatenate([gq, gk, gv, gg, zf, zb, zpad, dq, dk, dv], axis=1).astype(BF16)
    wlr = jnp.zeros((LANES, 2 * GLA_QK), F32)
    wlr = wlr.at[0:GLA_RANK, 0:GLA_QK].set(w_lr_f[0])
    wlr = wlr.at[GLA_RANK:2 * GLA_RANK, GLA_QK:].set(w_lr_b[0])
    return dict(
        w_all=w_all,
        wlr=wlr.astype(BF16),
        blr=jnp.concatenate([b_lr_f[0], b_lr_b[0]])[None, :],
        gla_g=gla_norm_g[0][None, :],
        lam4=jnp.stack([lambda_q1[0], lambda_k1[0], lambda_q2[0], lambda_k2[0]]),
        diff_g=diff_norm_g[0][None, :],
        w_o=w_o[0].astype(BF16),
        ln1_g=ln1_g[0][None, :], ln1_b=ln1_b[0][None, :],
        w_router=w_router[0], w_router_t=w_router[0].T,
        w_gate=w_gate[0].astype(BF16), w_up=w_up[0].astype(BF16), w_down=w_down[0].astype(BF16),
        ln2_g=ln2_g[0][None, :], ln2_b=ln2_b[0][None, :],
    )


def kernel(x_prompt, x_sample, w_in, w_lr_f, b_lr_f, w_lr_b, b_lr_b, gla_norm_g,
           lambda_q1, lambda_k1, lambda_q2, lambda_k2, diff_norm_g, w_o,
           ln1_g, ln1_b, w_router, w_gate, w_up, w_down, ln2_g, ln2_b):
    wts = _prep_weights(w_in, w_lr_f, b_lr_f, w_lr_b, b_lr_b, gla_norm_g,
                        lambda_q1, lambda_k1, lambda_q2, lambda_k2, diff_norm_g, w_o,
                        ln1_g, ln1_b, w_router, w_gate, w_up, w_down, ln2_g, ln2_b)
    return (_encoder_group(x_prompt, wts), _encoder_group(x_sample, wts))
```

```python
import functools
import math

import jax
import jax.numpy as jnp
from jax import lax
from jax.experimental import pallas as pl
from jax.experimental.pallas import tpu as pltpu

F32 = jnp.float32
BF16 = jnp.bfloat16
I32 = jnp.int32

D_MODEL = 1024
GLA_HEADS = 4
GLA_DK = 64
GLA_DV = 128
GLA_RANK = 16
GLA_GATE_TAU = 16.0
GLA_CHUNK = 64
GLA_QK = GLA_HEADS * GLA_DK
GLA_WIDTH = GLA_HEADS * GLA_DV
DIFF_HEADS = 4
DIFF_DV = 128
DIFF_DH = 64
DIFF_WIDTH = DIFF_HEADS * DIFF_DV
ROT_DIM = DIFF_DH // 4
ROPE_THETA = 500000.0
N_EXPERTS = 16
CAPACITY_FACTOR = 2
EXPERT_FF = 2816
DEPTH = 1
ALPHA = (2 * DEPTH) ** 0.25
EPS = 1e-5
LAMBDA_INIT = 0.8 - 0.6 * math.exp(-0.3 * 0)

LANES = 128
BF16_SUBLANES = 16
VMEM_LIMIT = 56 * 1024 * 1024

ROW_TILE = 512
RANK_SUB = 512
GLA_BLOCK = 256
SLOT_CHUNK = 256
FFN_TILE = 512
FFN_FCHUNK = 256
VT_ROWS = DIFF_DV + BF16_SUBLANES
LOG2E = math.log2(math.e)

C_QKV = 0
C_GG = 1024
C_Z = 1536
C_DQK = 1664
C_DV = 2688
W_IN_COLS = 3200


def _nt_dot(a, b):
    return lax.dot_general(a, b, (((1,), (1,)), ((), ())), preferred_element_type=F32)


def _tn_dot(a, b):
    return lax.dot_general(a, b, (((0,), (0,)), ((), ())), preferred_element_type=F32)


def _dot(a, b):
    return jnp.dot(a, b, preferred_element_type=F32)


def _split3(x):
    hi = x.astype(BF16)
    r1 = x - hi.astype(F32)
    md = r1.astype(BF16)
    lo = (r1 - md.astype(F32)).astype(BF16)
    return hi, md, lo


def _layer_norm(v, g, b):
    mu = jnp.mean(v, axis=-1, keepdims=True)
    c = v - mu
    var = jnp.mean(c * c, axis=-1, keepdims=True)
    return c * lax.rsqrt(var + EPS) * g + b


def _silu(v):
    return v / (1.0 + jnp.exp(-v))


def _inproj_kernel(x_ref, w_ref, wlr_ref, blr_ref, ra_ref, rb_ref, rc_ref,
                   qkv_ref, gg_ref, la_ref, dq_ref, dk_ref, dv_ref):
    xb = x_ref[...].astype(BF16)

    def mm(lo, width):
        return _dot(xb, w_ref[:, lo:lo + width])

    g = mm(C_QKV, 1024)
    qkv_ref[:, 0:GLA_QK] = g[:, 0:GLA_QK] * (GLA_DK ** -0.5)
    qkv_ref[:, GLA_QK:1024] = g[:, GLA_QK:1024]
    gg_ref[...] = mm(C_GG, GLA_WIDTH)

    z = mm(C_Z, LANES).astype(BF16)
    pre = _dot(z, wlr_ref[...]) + blr_ref[...]
    log_sig = jnp.minimum(pre, 0.0) - jnp.log(1.0 + jnp.exp(-jnp.abs(pre)))
    la_ref[...] = log_sig * (1.0 / GLA_GATE_TAU)

    d = mm(C_DQK, 2 * DIFF_WIDTH)
    ra = ra_ref[...]
    rb = rb_ref[...]
    rc = rc_ref[...]
    for hh in range(2 * DIFF_HEADS):
        t = d[:, hh * LANES:(hh + 1) * LANES]
        y = t * ra + pltpu.roll(t, ROT_DIM // 2, 1) * rb + pltpu.roll(t, LANES - ROT_DIM // 2, 1) * rc
        if hh < DIFF_HEADS:
            dq_ref[:, hh * LANES:(hh + 1) * LANES] = (y * (LOG2E * DIFF_DH ** -0.5)).astype(BF16)
        else:
            h2 = hh - DIFF_HEADS
            dk_ref[:, h2 * LANES:(h2 + 1) * LANES] = y.astype(BF16)
    dv_ref[...] = mm(C_DV, DIFF_WIDTH).astype(BF16)


def _inproj(x2, w_all, wlr, blr, ra, rb, rc, seq_len):
    T = x2.shape[0]
    tm = ROW_TILE
    sblocks = seq_len // tm
    row = lambda i: (i, 0)
    const = lambda i: (0, 0)
    rope = lambda i: (i % sblocks, 0)
    return pl.pallas_call(
        _inproj_kernel,
        name="inproj",
        grid=(T // tm,),
        in_specs=[
            pl.BlockSpec((tm, D_MODEL), row),
            pl.BlockSpec((D_MODEL, W_IN_COLS), const),
            pl.BlockSpec((LANES, 2 * GLA_QK), const),
            pl.BlockSpec((1, 2 * GLA_QK), const),
            pl.BlockSpec((tm, LANES), rope),
            pl.BlockSpec((tm, LANES), rope),
            pl.BlockSpec((tm, LANES), rope),
        ],
        out_specs=[
            pl.BlockSpec((tm, 1024), row),
            pl.BlockSpec((tm, GLA_WIDTH), row),
            pl.BlockSpec((tm, 2 * GLA_QK), row),
            pl.BlockSpec((tm, DIFF_WIDTH), row),
            pl.BlockSpec((tm, DIFF_WIDTH), row),
            pl.BlockSpec((tm, DIFF_WIDTH), row),
        ],
        out_shape=[
            jax.ShapeDtypeStruct((T, 1024), F32),
            jax.ShapeDtypeStruct((T, GLA_WIDTH), F32),
            jax.ShapeDtypeStruct((T, 2 * GLA_QK), F32),
            jax.ShapeDtypeStruct((T, DIFF_WIDTH), BF16),
            jax.ShapeDtypeStruct((T, DIFF_WIDTH), BF16),
            jax.ShapeDtypeStruct((T, DIFF_WIDTH), BF16),
        ],
        compiler_params=pltpu.CompilerParams(
            dimension_semantics=("parallel",), vmem_limit_bytes=VMEM_LIMIT),
    )(x2, w_all, wlr, blr, ra, rb, rc)


def _gla_kernel(qkvf_ref, laf_ref, qkvb_ref, lab_ref, of_ref, ob_ref, sf_ref, sb_ref, *, nsub):
    @pl.when(pl.program_id(1) == 0)
    def _():
        sf_ref[...] = jnp.zeros(sf_ref.shape, F32)
        sb_ref[...] = jnp.zeros(sb_ref.shape, F32)

    L = GLA_CHUNK
    row = lax.broadcasted_iota(I32, (L, L), 0)
    col = lax.broadcasted_iota(I32, (L, L), 1)
    lower = row >= col
    upper = row <= col
    lane = lax.broadcasted_iota(I32, (1, GLA_QK), 1)
    hmask = [(lane >= h * GLA_DK) & (lane < (h + 1) * GLA_DK) for h in range(GLA_HEADS)]

    def chunk(qkv_ref, la_ref, o_ref, s_ref, c, keep, mid_row, end_row):
        r0 = c * L
        q = qkv_ref[r0:r0 + L, 0:GLA_QK]
        k = qkv_ref[r0:r0 + L, GLA_QK:2 * GLA_QK]
        vb = qkv_ref[r0:r0 + L, 2 * GLA_QK:1024].astype(BF16)
        la = la_ref[r0:r0 + L, :]
        tri = jnp.where(keep, 1.0, 0.0).astype(BF16)
        hi, md, lo = _split3(la)
        b = _dot(tri, hi) + _dot(tri, md) + _dot(tri, lo)
        bm = b[mid_row:mid_row + 1, :]
        be = b[end_row:end_row + 1, :]
        qs = q * jnp.exp(b - bm)
        ks = (k * jnp.exp(bm - b)).astype(BF16)
        kd = (k * jnp.exp(be - b)).astype(BF16)
        qe = q * jnp.exp(b)
        dec = jnp.exp(be)
        qstack = jnp.concatenate([jnp.where(hmask[h], qs, 0.0) for h in range(GLA_HEADS)],
                                 axis=0).astype(BF16)
        sc = _nt_dot(qstack, ks)
        for h in range(GLA_HEADS):
            s_h = jnp.where(keep, sc[h * L:(h + 1) * L, :], 0.0).astype(BF16)
            v_h = vb[:, h * GLA_DV:(h + 1) * GLA_DV]
            st = s_ref[h]
            qe_h = jnp.where(hmask[h], qe, 0.0).astype(BF16)
            o_ref[r0:r0 + L, h * GLA_DV:(h + 1) * GLA_DV] = (
                _dot(s_h, v_h) + _nt_dot(qe_h, st.astype(BF16)))
            s_ref[h] = st * dec + _tn_dot(v_h, kd)

    for c in range(nsub):
        chunk(qkvf_ref, laf_ref, of_ref, sf_ref, c, lower, L // 2 - 1, L - 1)
    for c in range(nsub - 1, -1, -1):
        chunk(qkvb_ref, lab_ref, ob_ref, sb_ref, c, upper, L // 2, 0)


def _gla(qkv, la):
    B, S, _ = qkv.shape
    gb = GLA_BLOCK
    n = S // gb
    fwd = lambda b, i: (b, i, 0)
    bwd = lambda b, i: (b, n - 1 - i, 0)
    bwd_la = lambda b, i: (b, n - 1 - i, 1)
    return pl.pallas_call(
        functools.partial(_gla_kernel, nsub=gb // GLA_CHUNK),
        name="gla",
        grid=(B, n),
        in_specs=[
            pl.BlockSpec((None, gb, 1024), fwd),
            pl.BlockSpec((None, gb, GLA_QK), fwd),
            pl.BlockSpec((None, gb, 1024), bwd),
            pl.BlockSpec((None, gb, GLA_QK), bwd_la),
        ],
        out_specs=[
            pl.BlockSpec((None, gb, GLA_WIDTH), fwd),
            pl.BlockSpec((None, gb, GLA_WIDTH), bwd),
        ],
        out_shape=[jax.ShapeDtypeStruct((B, S, GLA_WIDTH), F32)] * 2,
        scratch_shapes=[pltpu.VMEM((GLA_HEADS, GLA_DV, GLA_QK), F32)] * 2,
        compiler_params=pltpu.CompilerParams(
            dimension_semantics=("parallel", "arbitrary"), vmem_limit_bytes=VMEM_LIMIT),
    )(qkv, la, qkv, la)


def _attn_kernel(lam_ref, g_ref, q_ref, k_ref, vt_ref, o_ref,
                 q1_ref, q2_ref, m1_ref, a1_ref, m2_ref, a2_ref, *, qc):
    ki = pl.program_id(3)
    qb = q_ref.shape[0]

    @pl.when(ki == 0)
    def _():
        q = q_ref[...]
        lane = lax.broadcasted_iota(I32, (1, LANES), 1)
        zero = jnp.zeros_like(q)
        q1_ref[...] = jnp.where(lane < DIFF_DH, q, zero)
        q2_ref[...] = jnp.where(lane >= DIFF_DH, q, zero)
        for m_ref, a_ref in ((m1_ref, a1_ref), (m2_ref, a2_ref)):
            m_ref[...] = jnp.full(m_ref.shape, -jnp.inf, F32)
            a_ref[...] = jnp.zeros(a_ref.shape, F32)

    k = k_ref[...]
    vt = vt_ref[...]
    units = [(c, refs) for c in range(qb // qc)
             for refs in ((q1_ref, m1_ref, a1_ref), (q2_ref, m2_ref, a2_ref))]
    live = {}

    def scores(i):
        c, (qm_ref, m_ref, _) = units[i]
        cs = slice(c * qc, (c + 1) * qc)
        s = _nt_dot(k, qm_ref[cs, :])
        m_old = m_ref[:, cs]
        m_new = jnp.maximum(m_old, jnp.max(s, axis=0, keepdims=True))
        m_ref[:, cs] = m_new
        live[i] = (s, m_new, jnp.exp2(m_old - m_new))

    def probs(i):
        s, m_new, corr = live[i]
        live[i] = (jnp.exp2(s - m_new).astype(BF16), corr)

    def values(i):
        c, (_, _, a_ref) = units[i]
        cs = slice(c * qc, (c + 1) * qc)
        p, corr = live.pop(i)
        a_ref[:, cs] = corr * a_ref[:, cs] + _dot(vt, p)

    n = len(units)
    for i in range(n + 2):
        if i < n:
            scores(i)
        if 0 <= i - 1 < n:
            probs(i - 1)
        if 0 <= i - 2 < n:
            values(i - 2)

    @pl.when(ki == pl.num_programs(3) - 1)
    def _():
        lp = lam_ref[...]
        lam = (jnp.exp(jnp.sum(lp[0:1] * lp[1:2], axis=1, keepdims=True))
               - jnp.exp(jnp.sum(lp[2:3] * lp[3:4], axis=1, keepdims=True)) + LAMBDA_INIT)
        o = (a1_ref[0:DIFF_DV, :] / a1_ref[DIFF_DV:DIFF_DV + 1, :]
             - lam * (a2_ref[0:DIFF_DV, :] / a2_ref[DIFF_DV:DIFF_DV + 1, :]))
        ot = o.T
        r = ot * lax.rsqrt(jnp.mean(ot * ot, axis=1, keepdims=True) + EPS) * g_ref[...]
        o_ref[...] = (r * (1.0 - LAMBDA_INIT)).astype(BF16)


def _attn(lam4, g, dq, dk, dvt, qb, kb, qc):
    B, S, _ = dq.shape
    qmap = lambda b, h, qi, ki: (b, qi, h)
    kmap = lambda b, h, qi, ki: (b, ki, h)
    vmap = lambda b, h, qi, ki: (b, h, ki)
    const = lambda b, h, qi, ki: (0, 0)
    return pl.pallas_call(
        functools.partial(_attn_kernel, qc=qc),
        name="attn",
        grid=(B, DIFF_HEADS, S // qb, S // kb),
        in_specs=[
            pl.BlockSpec((4, DIFF_DH), const),
            pl.BlockSpec((1, DIFF_DV), const),
            pl.BlockSpec((None, qb, LANES), qmap),
            pl.BlockSpec((None, kb, LANES), kmap),
            pl.BlockSpec((None, VT_ROWS, kb), vmap),
        ],
        out_specs=pl.BlockSpec((None, qb, LANES), qmap),
        out_shape=jax.ShapeDtypeStruct((B, S, DIFF_WIDTH), BF16),
        scratch_shapes=[
            pltpu.VMEM((qb, LANES), BF16), pltpu.VMEM((qb, LANES), BF16),
            pltpu.VMEM((1, qb), F32), pltpu.VMEM((VT_ROWS, qb), F32),
            pltpu.VMEM((1, qb), F32), pltpu.VMEM((VT_ROWS, qb), F32),
        ],
        compiler_params=pltpu.CompilerParams(
            dimension_semantics=("parallel", "parallel", "parallel", "arbitrary"),
            vmem_limit_bytes=VMEM_LIMIT),
    )(lam4, g, dq, dk, dvt)


def _post_kernel(of_ref, ob_ref, gg_ref, md_ref, x_ref, gn_ref, wo_ref, g1_ref, b1_ref,
                 wr_ref, wrt_ref, h_ref, hb_ref, aff_ref, afft_ref):
    gn = gn_ref[...]
    parts = []
    for h in range(GLA_HEADS):
        sl = slice(h * GLA_DV, (h + 1) * GLA_DV)
        o = of_ref[:, sl] + ob_ref[:, sl]
        r = o * lax.rsqrt(jnp.mean(o * o, axis=1, keepdims=True) + EPS) * gn
        parts.append((r * _silu(gg_ref[:, sl])).astype(BF16))
    mixed_gla = jnp.concatenate(parts, axis=1)
    mix = _dot(mixed_gla, wo_ref[0:GLA_WIDTH, :]) + _dot(md_ref[...], wo_ref[GLA_WIDTH:, :])
    hval = _layer_norm(ALPHA * x_ref[...] + mix, g1_ref[...], b1_ref[...])
    h_ref[...] = hval
    hh = hval.astype(BF16)
    hb_ref[...] = hh
    hl = (hval - hh.astype(F32)).astype(BF16)
    wr = wr_ref[...]
    wh = wr.astype(BF16)
    wl = (wr - wh.astype(F32)).astype(BF16)
    logits = _dot(hh, wh) + _dot(hl, wh) + _dot(hh, wl)
    e = jnp.exp(logits - jnp.max(logits, axis=1, keepdims=True))
    aff_ref[...] = e / jnp.sum(e, axis=1, keepdims=True)
    wrt = wrt_ref[...]
    wth = wrt.astype(BF16)
    wtl = (wrt - wth.astype(F32)).astype(BF16)
    lt = _nt_dot(wth, hh) + _nt_dot(wth, hl) + _nt_dot(wtl, hh)
    et = jnp.exp(lt - jnp.max(lt, axis=0, keepdims=True))
    afft_ref[...] = et / jnp.sum(et, axis=0, keepdims=True)


def _post(o_f, o_b, gg, md, x2, gn, wo, g1, b1, wr, wrt):
    T = x2.shape[0]
    tm = ROW_TILE
    assert tm == RANK_SUB
    row = lambda i: (i, 0)
    const = lambda i: (0, 0)
    return pl.pallas_call(
        _post_kernel,
        name="post",
        grid=(T // tm,),
        in_specs=[
            pl.BlockSpec((tm, GLA_WIDTH), row),
            pl.BlockSpec((tm, GLA_WIDTH), row),
            pl.BlockSpec((tm, GLA_WIDTH), row),
            pl.BlockSpec((tm, DIFF_WIDTH), row),
            pl.BlockSpec((tm, D_MODEL), row),
            pl.BlockSpec((1, GLA_DV), const),
            pl.BlockSpec((D_MODEL, D_MODEL), const),
            pl.BlockSpec((1, D_MODEL), const),
            pl.BlockSpec((1, D_MODEL), const),
            pl.BlockSpec((D_MODEL, N_EXPERTS), const),
            pl.BlockSpec((N_EXPERTS, D_MODEL), const),
        ],
        out_specs=[
            pl.BlockSpec((tm, D_MODEL), row),
            pl.BlockSpec((tm, D_MODEL), row),
            pl.BlockSpec((tm, N_EXPERTS), row),
            pl.BlockSpec((None, N_EXPERTS, tm), lambda i: (i, 0, 0)),
        ],
        out_shape=[
            jax.ShapeDtypeStruct((T, D_MODEL), F32),
            jax.ShapeDtypeStruct((T, D_MODEL), BF16),
            jax.ShapeDtypeStruct((T, N_EXPERTS), F32),
            jax.ShapeDtypeStruct((T // tm, N_EXPERTS, tm), F32),
        ],
        compiler_params=pltpu.CompilerParams(
            dimension_semantics=("parallel",), vmem_limit_bytes=VMEM_LIMIT),
    )(o_f, o_b, gg, md, x2, gn, wo, g1, b1, wr, wrt)


def _route_kernel(aff_ref, lr_ref, cnt_ref, off_ref, tot_ref, *, cap, nblk, spb):
    E = N_EXPERTS

    def count(mask):
        part = jnp.sum(jnp.where(mask, 1, 0).astype(I32), axis=0)
        return jnp.sum(part, axis=1, keepdims=True)

    def search(i, thr):
        cand = thr | jnp.left_shift(jnp.int32(1), 30 - i)
        bits = pltpu.bitcast(aff_ref[...], I32)
        return jnp.where(count(bits >= cand[None]) >= cap, cand, thr)

    thr = lax.fori_loop(0, 31, search, jnp.zeros((E, 1), I32))
    bits_all = pltpu.bitcast(aff_ref[...], I32)
    need = cap - count(bits_all > thr[None])

    r = lax.broadcasted_iota(I32, (RANK_SUB, RANK_SUB), 0)
    c = lax.broadcasted_iota(I32, (RANK_SUB, RANK_SUB), 1)
    before = jnp.where(r < c, 1.0, 0.0).astype(BF16)

    def block(j, carry):
        eqc, offc = carry
        selc = jnp.zeros((E, 1), I32)
        for s in range(spb):
            idx = j * spb + s
            bits = pltpu.bitcast(aff_ref[idx], I32)
            gt = bits > thr
            eq = bits == thr
            eqrank = _dot(jnp.where(eq, 1.0, 0.0).astype(BF16), before).astype(I32) + eqc
            sel = gt | (eq & (eqrank < need))
            rank = _dot(jnp.where(sel, 1.0, 0.0).astype(BF16), before).astype(I32) + selc
            lr_ref[idx] = jnp.where(sel, rank, -1)
            eqc = eqc + jnp.sum(jnp.where(eq, 1, 0).astype(I32), axis=1, keepdims=True)
            selc = selc + jnp.sum(jnp.where(sel, 1, 0).astype(I32), axis=1, keepdims=True)
        cnt_ref[j] = jnp.broadcast_to(selc, (E, LANES))
        off_ref[j] = jnp.broadcast_to(offc, (E, LANES))
        padded = (selc + (BF16_SUBLANES - 1)) & (-BF16_SUBLANES)
        return eqc, offc + padded

    zero = jnp.zeros((E, 1), I32)
    _, total = lax.fori_loop(0, nblk, block, (zero, zero))
    tot_ref[...] = jnp.broadcast_to(total, (E, LANES))


def _route(aff_blocks, cap, tb):
    nsb = aff_blocks.shape[0]
    spb = tb // RANK_SUB
    nblk = nsb // spb
    E = N_EXPERTS
    return pl.pallas_call(
        functools.partial(_route_kernel, cap=cap, nblk=nblk, spb=spb),
        name="route",
        out_shape=[
            jax.ShapeDtypeStruct((nsb, E, RANK_SUB), I32),
            jax.ShapeDtypeStruct((nblk, E, LANES), I32),
            jax.ShapeDtypeStruct((nblk, E, LANES), I32),
            jax.ShapeDtypeStruct((E, LANES), I32),
        ],
        compiler_params=pltpu.CompilerParams(vmem_limit_bytes=VMEM_LIMIT),
    )(aff_blocks)


def _gather_kernel(cnt_s, off_s, h_ref, lr_ref, xe_in, xe_out, xbuf, sem, *, spb):
    del xe_in
    j = pl.program_id(0)
    E = N_EXPERTS
    RC = SLOT_CHUNK

    def nchunks(e):
        return (cnt_s[j * E + e] + (RC - 1)) // RC

    def copy(e, slot, base):
        off = off_s[j * E + e]
        return pltpu.make_async_copy(
            xbuf.at[slot, pl.ds(base, RC)],
            xe_out.at[e, pl.ds(pl.multiple_of(off + base, BF16_SUBLANES), RC)],
            sem.at[slot])

    def drain(e, slot):
        def body(rc, carry):
            copy(e, slot, pl.multiple_of(rc * RC, RC)).wait()
            return carry
        lax.fori_loop(0, nchunks(e), body, 0)

    for e in range(E):
        slot = e % 2
        if e >= 2:
            drain(e - 2, slot)

        def body(rc, carry, e=e, slot=slot):
            base = pl.multiple_of(rc * RC, RC)
            rows = base + lax.broadcasted_iota(I32, (RC, 1), 0)
            acc = jnp.zeros((RC, D_MODEL), F32)
            for s in range(spb):
                lr = lr_ref[s, e:e + 1, :]
                onehot = jnp.where(lr == rows, 1.0, 0.0).astype(BF16)
                acc = acc + _dot(onehot, h_ref[s * RANK_SUB:(s + 1) * RANK_SUB, :])
            xbuf[slot, pl.ds(base, RC), :] = acc.astype(BF16)
            copy(e, slot, base).start()
            return carry
        lax.fori_loop(0, nchunks(e), body, 0)

    drain(E - 2, 0)
    drain(E - 1, 1)


def _gather(cnt_flat, off_flat, hb, lr, cp, tb):
    T = hb.shape[0]
    spb = tb // RANK_SUB
    E = N_EXPERTS
    xe0 = jnp.zeros((E, cp, D_MODEL), BF16)
    return pl.pallas_call(
        functools.partial(_gather_kernel, spb=spb),
        name="gather",
        grid_spec=pltpu.PrefetchScalarGridSpec(
            num_scalar_prefetch=2,
            grid=(T // tb,),
            in_specs=[
                pl.BlockSpec((tb, D_MODEL), lambda j, c, o: (j, 0)),
                pl.BlockSpec((spb, E, RANK_SUB), lambda j, c, o: (j, 0, 0)),
                pl.BlockSpec(memory_space=pl.ANY),
            ],
            out_specs=pl.BlockSpec(memory_space=pl.ANY),
            scratch_shapes=[pltpu.VMEM((2, tb, D_MODEL), BF16), pltpu.SemaphoreType.DMA((2,))],
        ),
        out_shape=jax.ShapeDtypeStruct((E, cp, D_MODEL), BF16),
        input_output_aliases={4: 0},
        compiler_params=pltpu.CompilerParams(
            dimension_semantics=("arbitrary",), vmem_limit_bytes=VMEM_LIMIT),
    )(cnt_flat, off_flat, hb, lr, xe0)


def _ffn_kernel(tot_s, x_ref, wg_ref, wu_ref, wd_ref, y_ref, acc_ref):
    e = pl.program_id(0)
    i = pl.program_id(1)
    tm = x_ref.shape[0]
    live = i * tm < tot_s[e]

    @pl.when(live)
    def _():
        x = x_ref[...]
        for c in range(EXPERT_FF // FFN_FCHUNK):
            cs = slice(c * FFN_FCHUNK, (c + 1) * FFN_FCHUNK)
            g = _dot(x, wg_ref[:, cs])
            u = _dot(x, wu_ref[:, cs])
            hid = (_silu(g) * u).astype(BF16)
            part = _dot(hid, wd_ref[cs, :])
            if c == 0:
                acc_ref[...] = part
            else:
                acc_ref[...] += part
        y_ref[...] = acc_ref[...].astype(BF16)

    @pl.when(jnp.logical_not(live))
    def _():
        y_ref[...] = jnp.zeros(y_ref.shape, BF16)


def _ffn(tot, xe, wg, wu, wd):
    E, cp, _ = xe.shape
    tm = FFN_TILE
    return pl.pallas_call(
        _ffn_kernel,
        name="ffn",
        grid_spec=pltpu.PrefetchScalarGridSpec(
            num_scalar_prefetch=1,
            grid=(E, cp // tm),
            in_specs=[
                pl.BlockSpec((None, tm, D_MODEL), lambda e, i, t: (e, i, 0)),
                pl.BlockSpec((None, D_MODEL, EXPERT_FF), lambda e, i, t: (e, 0, 0)),
                pl.BlockSpec((None, D_MODEL, EXPERT_FF), lambda e, i, t: (e, 0, 0)),
                pl.BlockSpec((None, EXPERT_FF, D_MODEL), lambda e, i, t: (e, 0, 0)),
            ],
            out_specs=pl.BlockSpec((None, tm, D_MODEL), lambda e, i, t: (e, i, 0)),
            scratch_shapes=[pltpu.VMEM((tm, D_MODEL), F32)],
        ),
        out_shape=jax.ShapeDtypeStruct((E, cp, D_MODEL), BF16),
        compiler_params=pltpu.CompilerParams(
            dimension_semantics=("parallel", "arbitrary"), vmem_limit_bytes=VMEM_LIMIT),
    )(tot, xe, wg, wu, wd)


def _combine_kernel(cnt_s, off_s, ye_hbm, lrt_ref, aff_ref, h_ref, g2_ref, b2_ref, y_ref,
                    ybuf, sem, *, spb):
    j = pl.program_id(0)
    E = N_EXPERTS
    RC = SLOT_CHUNK

    def nchunks(e):
        return (cnt_s[j * E + e] + (RC - 1)) // RC

    def copy(e, slot, base):
        off = off_s[j * E + e]
        return pltpu.make_async_copy(
            ye_hbm.at[e, pl.ds(pl.multiple_of(off + base, BF16_SUBLANES), RC)],
            ybuf.at[slot, pl.ds(base, RC)],
            sem.at[slot])

    def start(e, slot):
        def body(rc, carry):
            copy(e, slot, pl.multiple_of(rc * RC, RC)).start()
            return carry
        lax.fori_loop(0, nchunks(e), body, 0)

    def wait(e, slot):
        def body(rc, carry):
            copy(e, slot, pl.multiple_of(rc * RC, RC)).wait()
            return carry
        lax.fori_loop(0, nchunks(e), body, 0)

    start(0, 0)
    y_ref[...] = jnp.zeros(y_ref.shape, F32)
    for e in range(E):
        slot = e % 2
        if e + 1 < E:
            start(e + 1, 1 - slot)
        wait(e, slot)

        def body(rc, carry, e=e, slot=slot):
            base = pl.multiple_of(rc * RC, RC)
            ye = ybuf[slot, pl.ds(base, RC), :]
            cols = base + lax.broadcasted_iota(I32, (1, RC), 1)
            for s in range(spb):
                ts = slice(s * RANK_SUB, (s + 1) * RANK_SUB)
                lr = lrt_ref[ts, e:e + 1]
                onehot = jnp.where(lr == cols, 1.0, 0.0).astype(BF16)
                y_ref[ts, :] += _dot(onehot, ye) * aff_ref[ts, e:e + 1]
            return carry
        lax.fori_loop(0, nchunks(e), body, 0)

    y_ref[...] = _layer_norm(ALPHA * h_ref[...] + y_ref[...], g2_ref[...], b2_ref[...])


def _combine(cnt_flat, off_flat, ye, lrt, aff, h, g2, b2, tb):
    T = h.shape[0]
    spb = tb // RANK_SUB
    E = N_EXPERTS
    return pl.pallas_call(
        functools.partial(_combine_kernel, spb=spb),
        name="combine",
        grid_spec=pltpu.PrefetchScalarGridSpec(
            num_scalar_prefetch=2,
            grid=(T // tb,),
            in_specs=[
                pl.BlockSpec(memory_space=pl.ANY),
                pl.BlockSpec((tb, E), lambda j, c, o: (j, 0)),
                pl.BlockSpec((tb, E), lambda j, c, o: (j, 0)),
                pl.BlockSpec((tb, D_MODEL), lambda j, c, o: (j, 0)),
                pl.BlockSpec((1, D_MODEL), lambda j, c, o: (0, 0)),
                pl.BlockSpec((1, D_MODEL), lambda j, c, o: (0, 0)),
            ],
            out_specs=pl.BlockSpec((tb, D_MODEL), lambda j, c, o: (j, 0)),
            scratch_shapes=[pltpu.VMEM((2, tb, D_MODEL), BF16), pltpu.SemaphoreType.DMA((2,))],
        ),
        out_shape=jax.ShapeDtypeStruct((T, D_MODEL), F32),
        compiler_params=pltpu.CompilerParams(
            dimension_semantics=("arbitrary",), vmem_limit_bytes=VMEM_LIMIT),
    )(cnt_flat, off_flat, ye, lrt, aff, h, g2, b2)


def _rope_tables(seq_len):
    half = ROT_DIM // 2
    inv_freq = ROPE_THETA ** (-jnp.arange(0, ROT_DIM, 2, dtype=F32) / ROT_DIM)
    ang = jnp.arange(seq_len, dtype=F32)[:, None] * inv_freq[None, :]
    cos = jnp.cos(ang)
    sin = jnp.sin(ang)
    ones = jnp.ones((seq_len, DIFF_DH - ROT_DIM), F32)
    zeros = jnp.zeros((seq_len, DIFF_DH - ROT_DIM), F32)
    zh = jnp.zeros((seq_len, half), F32)
    ra = jnp.concatenate([cos, cos, ones], axis=1)
    rb = jnp.concatenate([zh, sin, zeros], axis=1)
    rc = jnp.concatenate([-sin, zh, zeros], axis=1)
    two = lambda t: jnp.concatenate([t, t], axis=1)
    return two(ra), two(rb), two(rc)


def _moe_tiles(T):
    tb = min(2048, T // 2)
    cap = (CAPACITY_FACTOR * T) // N_EXPERTS
    nblk = T // tb
    cp = cap + BF16_SUBLANES * nblk + SLOT_CHUNK
    cp = -(-cp // FFN_TILE) * FFN_TILE
    return tb, cap, cp


def _attn_tiles(S):
    return min(1024, S), min(1024, S), min(512, S)


def _values_transposed(dv, B, S):
    vt = dv.reshape(B, S, DIFF_HEADS, DIFF_DV).transpose(0, 2, 3, 1)
    ones = jnp.ones((B, DIFF_HEADS, VT_ROWS - DIFF_DV, S), BF16)
    return jnp.concatenate([vt, ones], axis=2).reshape(B, DIFF_HEADS * VT_ROWS, S)


def _encoder_group(x, wts):
    B, S, D = x.shape
    T = B * S
    x2 = x.reshape(T, D)
    ra, rb, rc = _rope_tables(S)
    qkv, gg, la, dq, dk, dv = _inproj(x2, wts["w_all"], wts["wlr"], wts["blr"], ra, rb, rc, S)
    o_f, o_b = _gla(qkv.reshape(B, S, 1024), la.reshape(B, S, 2 * GLA_QK))
    qb, kb, qc = _attn_tiles(S)
    md = _attn(wts["lam4"], wts["diff_g"], dq.reshape(B, S, DIFF_WIDTH),
               dk.reshape(B, S, DIFF_WIDTH), _values_transposed(dv, B, S), qb, kb, qc)
    h, hb, aff, aff_blocks = _post(
        o_f.reshape(T, GLA_WIDTH), o_b.reshape(T, GLA_WIDTH), gg, md.reshape(T, DIFF_WIDTH), x2,
        wts["gla_g"], wts["w_o"], wts["ln1_g"], wts["ln1_b"], wts["w_router"], wts["w_router_t"])
    tb, cap, cp = _moe_tiles(T)
    lr, cnt, off, tot = _route(aff_blocks, cap, tb)
    cnt_flat = cnt[:, :, 0].reshape(-1)
    off_flat = off[:, :, 0].reshape(-1)
    lrt = lr.transpose(0, 2, 1).reshape(T, N_EXPERTS)
    xe = _gather(cnt_flat, off_flat, hb, lr, cp, tb)
    ye = _ffn(tot[:, 0], xe, wts["w_gate"], wts["w_up"], wts["w_down"])
    y = _combine(cnt_flat, off_flat, ye, lrt, aff, h, wts["ln2_g"], wts["ln2_b"], tb)
    return y.reshape(B, S, D)


def _prep_weights(w_in, w_lr_f, b_lr_f, w_lr_b, b_lr_b, gla_norm_g,
                  lambda_q1, lambda_k1, lambda_q2, lambda_k2, diff_norm_g, w_o,
                  ln1_g, ln1_b, w_router, w_gate, w_up, w_down, ln2_g, ln2_b):
    sizes = (GLA_QK, GLA_QK, GLA_WIDTH, GLA_WIDTH, GLA_RANK, GLA_RANK,
             DIFF_WIDTH, DIFF_WIDTH, DIFF_WIDTH)
    cols = []
    start = 0
    for s in sizes:
        cols.append(w_in[0][:, start:start + s])
        start += s
    gq, gk, gv, gg, zf, zb, dq, dk, dv = cols
    zpad = jnp.zeros((D_MODEL, LANES - 2 * GLA_RANK), F32)
    w_all = jnp.concatenate([gq, gk, gv, gg, zf, zb, zpad, dq, dk, dv], axis=1).astype(BF16)
    wlr = jnp.zeros((LANES, 2 * GLA_QK), F32)
    wlr = wlr.at[0:GLA_RANK, 0:GLA_QK].set(w_lr_f[0])
    wlr = wlr.at[GLA_RANK:2 * GLA_RANK, GLA_QK:].set(w_lr_b[0])
    return dict(
        w_all=w_all,
        wlr=wlr.astype(BF16),
        blr=jnp.concatenate([b_lr_f[0], b_lr_b[0]])[None, :],
        gla_g=gla_norm_g[0][None, :],
        lam4=jnp.stack([lambda_q1[0], lambda_k1[0], lambda_q2[0], lambda_k2[0]]),
        diff_g=diff_norm_g[0][None, :],
        w_o=w_o[0].astype(BF16),
        ln1_g=ln1_g[0][None, :], ln1_b=ln1_b[0][None, :],
        w_router=w_router[0], w_router_t=w_router[0].T,
        w_gate=w_gate[0].astype(BF16), w_up=w_up[0].astype(BF16), w_down=w_down[0].astype(BF16),
        ln2_g=ln2_g[0][None, :], ln2_b=ln2_b[0][None, :],
    )


def kernel(x_prompt, x_sample, w_in, w_lr_f, b_lr_f, w_lr_b, b_lr_b, gla_norm_g,
           lambda_q1, lambda_k1, lambda_q2, lambda_k2, diff_norm_g, w_o,
           ln1_g, ln1_b, w_router, w_gate, w_up, w_down, ln2_g, ln2_b):
    wts = _prep_weights(w_in, w_lr_f, b_lr_f, w_lr_b, b_lr_b, gla_norm_g,
                        lambda_q1, lambda_k1, lambda_q2, lambda_k2, diff_norm_g, w_o,
                        ln1_g, ln1_b, w_router, w_gate, w_up, w_down, ln2_g, ln2_b)
    return (_encoder_group(x_prompt, wts), _encoder_group(x_sample, wts))
```

```python
import functools
import math

import jax
import jax.numpy as jnp
from jax import lax
from jax.experimental import pallas as pl
from jax.experimental.pallas import tpu as pltpu

F32 = jnp.float32
BF16 = jnp.bfloat16
I32 = jnp.int32

D_MODEL = 1024
GLA_HEADS = 4
GLA_DK = 64
GLA_DV = 128
GLA_RANK = 16
GLA_GATE_TAU = 16.0
GLA_CHUNK = 64
GLA_QK = GLA_HEADS * GLA_DK
GLA_WIDTH = GLA_HEADS * GLA_DV
DIFF_HEADS = 4
DIFF_DV = 128
DIFF_DH = 64
DIFF_WIDTH = DIFF_HEADS * DIFF_DV
ROT_DIM = DIFF_DH // 4
ROPE_THETA = 500000.0
N_EXPERTS = 16
CAPACITY_FACTOR = 2
EXPERT_FF = 2816
DEPTH = 1
ALPHA = (2 * DEPTH) ** 0.25
EPS = 1e-5
LAMBDA_INIT = 0.8 - 0.6 * math.exp(-0.3 * 0)

LANES = 128
BF16_SUBLANES = 16
VMEM_LIMIT = 56 * 1024 * 1024

ROW_TILE = 512
RANK_SUB = 512
GLA_BLOCK = 256
SLOT_ROWS = 128
XE_WIDTH = D_MODEL + LANES
FFN_TILE = 512
FFN_FCHUNK = 256
VT_ROWS = DIFF_DV + BF16_SUBLANES
LOG2E = math.log2(math.e)

C_QKV = 0
C_GG = 1024
C_Z = 1536
C_DQK = 1664
C_DV = 2688
W_IN_COLS = 3200


def _nt_dot(a, b):
    return lax.dot_general(a, b, (((1,), (1,)), ((), ())), preferred_element_type=F32)


def _tn_dot(a, b):
    return lax.dot_general(a, b, (((0,), (0,)), ((), ())), preferred_element_type=F32)


def _dot(a, b):
    return jnp.dot(a, b, preferred_element_type=F32)


def _split3(x):
    hi = x.astype(BF16)
    r1 = x - hi.astype(F32)
    md = r1.astype(BF16)
    lo = (r1 - md.astype(F32)).astype(BF16)
    return hi, md, lo


def _layer_norm(v, g, b):
    mu = jnp.mean(v, axis=-1, keepdims=True)
    c = v - mu
    var = jnp.mean(c * c, axis=-1, keepdims=True)
    return c * lax.rsqrt(var + EPS) * g + b


def _silu(v):
    return v / (1.0 + jnp.exp(-v))


def _inproj_kernel(x_ref, w_ref, wlr_ref, blr_ref, ra_ref, rb_ref, rc_ref,
                   qkv_ref, gg_ref, la_ref, dq_ref, dk_ref, dv_ref):
    xb = x_ref[...].astype(BF16)

    def mm(lo, width):
        return _dot(xb, w_ref[:, lo:lo + width])

    g = mm(C_QKV, 1024)
    qkv_ref[:, 0:GLA_QK] = g[:, 0:GLA_QK] * (GLA_DK ** -0.5)
    qkv_ref[:, GLA_QK:1024] = g[:, GLA_QK:1024]
    gg_ref[...] = mm(C_GG, GLA_WIDTH)

    z = mm(C_Z, LANES).astype(BF16)
    pre = _dot(z, wlr_ref[...]) + blr_ref[...]
    log_sig = jnp.minimum(pre, 0.0) - jnp.log(1.0 + jnp.exp(-jnp.abs(pre)))
    la_ref[...] = log_sig * (1.0 / GLA_GATE_TAU)

    d = mm(C_DQK, 2 * DIFF_WIDTH)
    ra = ra_ref[...]
    rb = rb_ref[...]
    rc = rc_ref[...]
    for hh in range(2 * DIFF_HEADS):
        t = d[:, hh * LANES:(hh + 1) * LANES]
        y = t * ra + pltpu.roll(t, ROT_DIM // 2, 1) * rb + pltpu.roll(t, LANES - ROT_DIM // 2, 1) * rc
        if hh < DIFF_HEADS:
            dq_ref[:, hh * LANES:(hh + 1) * LANES] = (y * (LOG2E * DIFF_DH ** -0.5)).astype(BF16)
        else:
            h2 = hh - DIFF_HEADS
            dk_ref[:, h2 * LANES:(h2 + 1) * LANES] = y.astype(BF16)
    dv_ref[...] = mm(C_DV, DIFF_WIDTH).astype(BF16)


def _inproj(x2, w_all, wlr, blr, ra, rb, rc, seq_len):
    T = x2.shape[0]
    tm = ROW_TILE
    sblocks = seq_len // tm
    row = lambda i: (i, 0)
    const = lambda i: (0, 0)
    rope = lambda i: (i % sblocks, 0)
    return pl.pallas_call(
        _inproj_kernel,
        name="inproj",
        grid=(T // tm,),
        in_specs=[
            pl.BlockSpec((tm, D_MODEL), row),
            pl.BlockSpec((D_MODEL, W_IN_COLS), const),
            pl.BlockSpec((LANES, 2 * GLA_QK), const),
            pl.BlockSpec((1, 2 * GLA_QK), const),
            pl.BlockSpec((tm, LANES), rope),
            pl.BlockSpec((tm, LANES), rope),
            pl.BlockSpec((tm, LANES), rope),
        ],
        out_specs=[
            pl.BlockSpec((tm, 1024), row),
            pl.BlockSpec((tm, GLA_WIDTH), row),
            pl.BlockSpec((tm, 2 * GLA_QK), row),
            pl.BlockSpec((tm, DIFF_WIDTH), row),
            pl.BlockSpec((tm, DIFF_WIDTH), row),
            pl.BlockSpec((tm, DIFF_WIDTH), row),
        ],
        out_shape=[
            jax.ShapeDtypeStruct((T, 1024), F32),
            jax.ShapeDtypeStruct((T, GLA_WIDTH), F32),
            jax.ShapeDtypeStruct((T, 2 * GLA_QK), F32),
            jax.ShapeDtypeStruct((T, DIFF_WIDTH), BF16),
            jax.ShapeDtypeStruct((T, DIFF_WIDTH), BF16),
            jax.ShapeDtypeStruct((T, DIFF_WIDTH), BF16),
        ],
        compiler_params=pltpu.CompilerParams(
            dimension_semantics=("parallel",), vmem_limit_bytes=VMEM_LIMIT),
    )(x2, w_all, wlr, blr, ra, rb, rc)


def _gla_kernel(qkvf_ref, laf_ref, qkvb_ref, lab_ref, of_ref, ob_ref, sf_ref, sb_ref, *, nsub):
    @pl.when(pl.program_id(1) == 0)
    def _():
        sf_ref[...] = jnp.zeros(sf_ref.shape, F32)
        sb_ref[...] = jnp.zeros(sb_ref.shape, F32)

    L = GLA_CHUNK
    row = lax.broadcasted_iota(I32, (L, L), 0)
    col = lax.broadcasted_iota(I32, (L, L), 1)
    lower = row >= col
    upper = row <= col
    lane = lax.broadcasted_iota(I32, (1, GLA_QK), 1)
    hmask = [(lane >= h * GLA_DK) & (lane < (h + 1) * GLA_DK) for h in range(GLA_HEADS)]

    def chunk(qkv_ref, la_ref, o_ref, s_ref, c, keep, mid_row, end_row):
        r0 = c * L
        q = qkv_ref[r0:r0 + L, 0:GLA_QK]
        k = qkv_ref[r0:r0 + L, GLA_QK:2 * GLA_QK]
        vb = qkv_ref[r0:r0 + L, 2 * GLA_QK:1024].astype(BF16)
        la = la_ref[r0:r0 + L, :]
        tri = jnp.where(keep, 1.0, 0.0).astype(BF16)
        hi, md, lo = _split3(la)
        b = _dot(tri, hi) + _dot(tri, md) + _dot(tri, lo)
        bm = b[mid_row:mid_row + 1, :]
        be = b[end_row:end_row + 1, :]
        qs = q * jnp.exp(b - bm)
        ks = (k * jnp.exp(bm - b)).astype(BF16)
        kd = (k * jnp.exp(be - b)).astype(BF16)
        qe = q * jnp.exp(b)
        dec = jnp.exp(be)
        qstack = jnp.concatenate([jnp.where(hmask[h], qs, 0.0) for h in range(GLA_HEADS)],
                                 axis=0).astype(BF16)
        sc = _nt_dot(qstack, ks)
        for h in range(GLA_HEADS):
            s_h = jnp.where(keep, sc[h * L:(h + 1) * L, :], 0.0).astype(BF16)
            v_h = vb[:, h * GLA_DV:(h + 1) * GLA_DV]
            st = s_ref[h]
            qe_h = jnp.where(hmask[h], qe, 0.0).astype(BF16)
            o_ref[r0:r0 + L, h * GLA_DV:(h + 1) * GLA_DV] = (
                _dot(s_h, v_h) + _nt_dot(qe_h, st.astype(BF16)))
            s_ref[h] = st * dec + _tn_dot(v_h, kd)

    for c in range(nsub):
        chunk(qkvf_ref, laf_ref, of_ref, sf_ref, c, lower, L // 2 - 1, L - 1)
    for c in range(nsub - 1, -1, -1):
        chunk(qkvb_ref, lab_ref, ob_ref, sb_ref, c, upper, L // 2, 0)


def _gla(qkv, la):
    B, S, _ = qkv.shape
    gb = GLA_BLOCK
    n = S // gb
    fwd = lambda b, i: (b, i, 0)
    bwd = lambda b, i: (b, n - 1 - i, 0)
    bwd_la = lambda b, i: (b, n - 1 - i, 1)
    return pl.pallas_call(
        functools.partial(_gla_kernel, nsub=gb // GLA_CHUNK),
        name="gla",
        grid=(B, n),
        in_specs=[
            pl.BlockSpec((None, gb, 1024), fwd),
            pl.BlockSpec((None, gb, GLA_QK), fwd),
            pl.BlockSpec((None, gb, 1024), bwd),
            pl.BlockSpec((None, gb, GLA_QK), bwd_la),
        ],
        out_specs=[
            pl.BlockSpec((None, gb, GLA_WIDTH), fwd),
            pl.BlockSpec((None, gb, GLA_WIDTH), bwd),
        ],
        out_shape=[jax.ShapeDtypeStruct((B, S, GLA_WIDTH), F32)] * 2,
        scratch_shapes=[pltpu.VMEM((GLA_HEADS, GLA_DV, GLA_QK), F32)] * 2,
        compiler_params=pltpu.CompilerParams(
            dimension_semantics=("parallel", "arbitrary"), vmem_limit_bytes=VMEM_LIMIT),
    )(qkv, la, qkv, la)


def _attn_kernel(lam_ref, g_ref, q_ref, k_ref, vt_ref, o_ref,
                 q1_ref, q2_ref, m1_ref, a1_ref, m2_ref, a2_ref, *, qc):
    ki = pl.program_id(3)
    qb = q_ref.shape[0]

    @pl.when(ki == 0)
    def _():
        q = q_ref[...]
        lane = lax.broadcasted_iota(I32, (1, LANES), 1)
        zero = jnp.zeros_like(q)
        q1_ref[...] = jnp.where(lane < DIFF_DH, q, zero)
        q2_ref[...] = jnp.where(lane >= DIFF_DH, q, zero)
        for m_ref, a_ref in ((m1_ref, a1_ref), (m2_ref, a2_ref)):
            m_ref[...] = jnp.full(m_ref.shape, -jnp.inf, F32)
            a_ref[...] = jnp.zeros(a_ref.shape, F32)

    k = k_ref[...]
    vt = vt_ref[...]
    units = [(c, refs) for c in range(qb // qc)
             for refs in ((q1_ref, m1_ref, a1_ref), (q2_ref, m2_ref, a2_ref))]
    live = {}

    def scores(i):
        c, (qm_ref, m_ref, _) = units[i]
        cs = slice(c * qc, (c + 1) * qc)
        s = _nt_dot(k, qm_ref[cs, :])
        m_old = m_ref[:, cs]
        m_new = jnp.maximum(m_old, jnp.max(s, axis=0, keepdims=True))
        m_ref[:, cs] = m_new
        live[i] = (s, m_new, jnp.exp2(m_old - m_new))

    def probs(i):
        s, m_new, corr = live[i]
        live[i] = (jnp.exp2(s - m_new).astype(BF16), corr)

    def values(i):
        c, (_, _, a_ref) = units[i]
        cs = slice(c * qc, (c + 1) * qc)
        p, corr = live.pop(i)
        a_ref[:, cs] = corr * a_ref[:, cs] + _dot(vt, p)

    n = len(units)
    for i in range(n + 2):
        if i < n:
            scores(i)
        if 0 <= i - 1 < n:
            probs(i - 1)
        if 0 <= i - 2 < n:
            values(i - 2)

    @pl.when(ki == pl.num_programs(3) - 1)
    def _():
        lp = lam_ref[...]
        lam = (jnp.exp(jnp.sum(lp[0:1] * lp[1:2], axis=1, keepdims=True))
               - jnp.exp(jnp.sum(lp[2:3] * lp[3:4], axis=1, keepdims=True)) + LAMBDA_INIT)
        o = (a1_ref[0:DIFF_DV, :] / a1_ref[DIFF_DV:DIFF_DV + 1, :]
             - lam * (a2_ref[0:DIFF_DV, :] / a2_ref[DIFF_DV:DIFF_DV + 1, :]))
        ot = o.T
        r = ot * lax.rsqrt(jnp.mean(ot * ot, axis=1, keepdims=True) + EPS) * g_ref[...]
        o_ref[...] = (r * (1.0 - LAMBDA_INIT)).astype(BF16)


def _attn(lam4, g, dq, dk, dvt, qb, kb, qc):
    B, S, _ = dq.shape
    qmap = lambda b, h, qi, ki: (b, qi, h)
    kmap = lambda b, h, qi, ki: (b, ki, h)
    vmap = lambda b, h, qi, ki: (b, h, ki)
    const = lambda b, h, qi, ki: (0, 0)
    return pl.pallas_call(
        functools.partial(_attn_kernel, qc=qc),
        name="attn",
        grid=(B, DIFF_HEADS, S // qb, S // kb),
        in_specs=[
            pl.BlockSpec((4, DIFF_DH), const),
            pl.BlockSpec((1, DIFF_DV), const),
            pl.BlockSpec((None, qb, LANES), qmap),
            pl.BlockSpec((None, kb, LANES), kmap),
            pl.BlockSpec((None, VT_ROWS, kb), vmap),
        ],
        out_specs=pl.BlockSpec((None, qb, LANES), qmap),
        out_shape=jax.ShapeDtypeStruct((B, S, DIFF_WIDTH), BF16),
        scratch_shapes=[
            pltpu.VMEM((qb, LANES), BF16), pltpu.VMEM((qb, LANES), BF16),
            pltpu.VMEM((1, qb), F32), pltpu.VMEM((VT_ROWS, qb), F32),
            pltpu.VMEM((1, qb), F32), pltpu.VMEM((VT_ROWS, qb), F32),
        ],
        compiler_params=pltpu.CompilerParams(
            dimension_semantics=("parallel", "parallel", "parallel", "arbitrary"),
            vmem_limit_bytes=VMEM_LIMIT),
    )(lam4, g, dq, dk, dvt)


def _post_kernel(of_ref, ob_ref, gg_ref, md_ref, x_ref, gn_ref, wo_ref, g1_ref, b1_ref,
                 wr_ref, wrt_ref, h_ref, hb_ref, g3_ref, afft_ref):
    gn = gn_ref[...]
    parts = []
    for h in range(GLA_HEADS):
        sl = slice(h * GLA_DV, (h + 1) * GLA_DV)
        o = of_ref[:, sl] + ob_ref[:, sl]
        r = o * lax.rsqrt(jnp.mean(o * o, axis=1, keepdims=True) + EPS) * gn
        parts.append((r * _silu(gg_ref[:, sl])).astype(BF16))
    mixed_gla = jnp.concatenate(parts, axis=1)
    mix = _dot(mixed_gla, wo_ref[0:GLA_WIDTH, :]) + _dot(md_ref[...], wo_ref[GLA_WIDTH:, :])
    hval = _layer_norm(ALPHA * x_ref[...] + mix, g1_ref[...], b1_ref[...])
    h_ref[...] = hval
    hh = hval.astype(BF16)
    hb_ref[...] = hh
    hl = (hval - hh.astype(F32)).astype(BF16)
    wr = wr_ref[...]
    wh = wr.astype(BF16)
    wl = (wr - wh.astype(F32)).astype(BF16)
    logits = _dot(hh, wh) + _dot(hl, wh) + _dot(hh, wl)
    e = jnp.exp(logits - jnp.max(logits, axis=1, keepdims=True))
    aff = e / jnp.sum(e, axis=1, keepdims=True)
    a_hi, a_md, a_lo = _split3(aff)
    pad = jnp.zeros((aff.shape[0], LANES - 3 * N_EXPERTS), F32)
    g3_ref[...] = jnp.concatenate(
        [a_hi.astype(F32), a_md.astype(F32), a_lo.astype(F32), pad], axis=1).astype(BF16)
    wrt = wrt_ref[...]
    wth = wrt.astype(BF16)
    wtl = (wrt - wth.astype(F32)).astype(BF16)
    lt = _nt_dot(wth, hh) + _nt_dot(wth, hl) + _nt_dot(wtl, hh)
    et = jnp.exp(lt - jnp.max(lt, axis=0, keepdims=True))
    afft_ref[...] = et / jnp.sum(et, axis=0, keepdims=True)


def _post(o_f, o_b, gg, md, x2, gn, wo, g1, b1, wr, wrt):
    T = x2.shape[0]
    tm = ROW_TILE
    assert tm == RANK_SUB
    row = lambda i: (i, 0)
    const = lambda i: (0, 0)
    return pl.pallas_call(
        _post_kernel,
        name="post",
        grid=(T // tm,),
        in_specs=[
            pl.BlockSpec((tm, GLA_WIDTH), row),
            pl.BlockSpec((tm, GLA_WIDTH), row),
            pl.BlockSpec((tm, GLA_WIDTH), row),
            pl.BlockSpec((tm, DIFF_WIDTH), row),
            pl.BlockSpec((tm, D_MODEL), row),
            pl.BlockSpec((1, GLA_DV), const),
            pl.BlockSpec((D_MODEL, D_MODEL), const),
            pl.BlockSpec((1, D_MODEL), const),
            pl.BlockSpec((1, D_MODEL), const),
            pl.BlockSpec((D_MODEL, N_EXPERTS), const),
            pl.BlockSpec((N_EXPERTS, D_MODEL), const),
        ],
        out_specs=[
            pl.BlockSpec((tm, D_MODEL), row),
            pl.BlockSpec((tm, D_MODEL), row),
            pl.BlockSpec((tm, LANES), row),
            pl.BlockSpec((None, N_EXPERTS, tm), lambda i: (i, 0, 0)),
        ],
        out_shape=[
            jax.ShapeDtypeStruct((T, D_MODEL), F32),
            jax.ShapeDtypeStruct((T, D_MODEL), BF16),
            jax.ShapeDtypeStruct((T, LANES), BF16),
            jax.ShapeDtypeStruct((T // tm, N_EXPERTS, tm), F32),
        ],
        compiler_params=pltpu.CompilerParams(
            dimension_semantics=("parallel",), vmem_limit_bytes=VMEM_LIMIT),
    )(o_f, o_b, gg, md, x2, gn, wo, g1, b1, wr, wrt)


def _route_kernel(aff_ref, lr_ref, cnt_ref, off_ref, tot_ref, *, cap, nblk, spb):
    E = N_EXPERTS

    def count(mask):
        part = jnp.sum(jnp.where(mask, 1, 0).astype(I32), axis=0)
        return jnp.sum(part, axis=1, keepdims=True)

    def search(i, thr):
        cand = thr | jnp.left_shift(jnp.int32(1), 30 - i)
        bits = pltpu.bitcast(aff_ref[...], I32)
        return jnp.where(count(bits >= cand[None]) >= cap, cand, thr)

    thr = lax.fori_loop(0, 31, search, jnp.zeros((E, 1), I32))
    bits_all = pltpu.bitcast(aff_ref[...], I32)
    need = cap - count(bits_all > thr[None])

    r = lax.broadcasted_iota(I32, (RANK_SUB, RANK_SUB), 0)
    c = lax.broadcasted_iota(I32, (RANK_SUB, RANK_SUB), 1)
    before = jnp.where(r < c, 1.0, 0.0).astype(BF16)

    def block(j, carry):
        eqc, offc = carry
        selc = jnp.zeros((E, 1), I32)
        for s in range(spb):
            idx = j * spb + s
            bits = pltpu.bitcast(aff_ref[idx], I32)
            gt = bits > thr
            eq = bits == thr
            eqrank = _dot(jnp.where(eq, 1.0, 0.0).astype(BF16), before).astype(I32) + eqc
            sel = gt | (eq & (eqrank < need))
            rank = _dot(jnp.where(sel, 1.0, 0.0).astype(BF16), before).astype(I32) + selc
            lr_ref[idx] = jnp.where(sel, rank, -1)
            eqc = eqc + jnp.sum(jnp.where(eq, 1, 0).astype(I32), axis=1, keepdims=True)
            selc = selc + jnp.sum(jnp.where(sel, 1, 0).astype(I32), axis=1, keepdims=True)
        cnt_ref[j] = jnp.broadcast_to(selc, (E, LANES))
        off_ref[j] = jnp.broadcast_to(offc, (E, LANES))
        padded = (selc + (BF16_SUBLANES - 1)) & (-BF16_SUBLANES)
        return eqc, offc + padded

    zero = jnp.zeros((E, 1), I32)
    _, total = lax.fori_loop(0, nblk, block, (zero, zero))
    tot_ref[...] = jnp.broadcast_to(total, (E, LANES))


def _route(aff_blocks, cap, tb):
    nsb = aff_blocks.shape[0]
    spb = tb // RANK_SUB
    nblk = nsb // spb
    E = N_EXPERTS
    return pl.pallas_call(
        functools.partial(_route_kernel, cap=cap, nblk=nblk, spb=spb),
        name="route",
        out_shape=[
            jax.ShapeDtypeStruct((nsb, E, RANK_SUB), I32),
            jax.ShapeDtypeStruct((nblk, E, LANES), I32),
            jax.ShapeDtypeStruct((nblk, E, LANES), I32),
            jax.ShapeDtypeStruct((E, LANES), I32),
        ],
        compiler_params=pltpu.CompilerParams(vmem_limit_bytes=VMEM_LIMIT),
    )(aff_blocks)


def _gather_kernel(cnt_s, off_s, h_ref, g3_ref, lr_ref, xe_out, haug, xbuf, obuf, zbuf, sem, osem,
                   *, cp):
    j = pl.program_id(0)
    nb = pl.num_programs(0)
    E = N_EXPERTS
    RS = SLOT_ROWS
    slot = j % 2

    haug[:, 0:D_MODEL] = h_ref[...]
    haug[:, D_MODEL:XE_WIDTH] = g3_ref[...]
    rows = lax.broadcasted_iota(I32, (RS, 1), 0)

    def onehot(e, base):
        return jnp.where(lr_ref[0, e:e + 1, :] == rows + base, 1.0, 0.0).astype(BF16)

    half = E // 2
    for g in range(2):
        stacked = jnp.concatenate([onehot(e, 0) for e in range(g * half, (g + 1) * half)], axis=0)
        xbuf[slot, g * half * RS:(g + 1) * half * RS, :] = _dot(stacked, haug[...]).astype(BF16)

    def copy(jj, e, s):
        off = off_s[jj * E + e]
        return pltpu.make_async_copy(
            xbuf.at[s, pl.ds(e * RS, RS)],
            xe_out.at[e, pl.ds(pl.multiple_of(off, BF16_SUBLANES), RS)],
            sem.at[s])

    @pl.when(j > 0)
    def _():
        for e in range(E):
            copy(j - 1, e, 1 - slot).wait()

    for e in range(E):
        copy(j, e, slot).start()

    for e in range(E):
        def extra(rc, carry, e=e):
            base = (rc + 1) * RS
            obuf[...] = _dot(onehot(e, base), haug[...]).astype(BF16)
            dst = pl.multiple_of(off_s[j * E + e] + base, BF16_SUBLANES)
            cpy = pltpu.make_async_copy(obuf, xe_out.at[e, pl.ds(dst, RS)], osem.at[0])
            cpy.start()
            cpy.wait()
            return carry
        lax.fori_loop(0, (cnt_s[j * E + e] + (RS - 1)) // RS - 1, extra, 0)

    @pl.when(j == nb - 1)
    def _():
        for e in range(E):
            copy(j, e, slot).wait()
        zbuf[...] = jnp.zeros(zbuf.shape, BF16)
        ZR = BF16_SUBLANES

        def zcopy(e, z0, i):
            return pltpu.make_async_copy(
                zbuf, xe_out.at[e, pl.ds(pl.multiple_of(z0 + i * ZR, ZR), ZR)], osem.at[0])

        for e in range(E):
            written = jnp.maximum((cnt_s[j * E + e] + (RS - 1)) // RS, 1) * RS
            z0 = off_s[j * E + e] + written
            n = (cp - z0) // ZR

            def zs(i, carry, e=e, z0=z0):
                zcopy(e, z0, i).start()
                return carry

            def zw(i, carry, e=e, z0=z0):
                zcopy(e, z0, i).wait()
                return carry
            lax.fori_loop(0, n, zs, 0)
            lax.fori_loop(0, n, zw, 0)


def _gather(cnt_flat, off_flat, hb, g3, lr, cp, tb):
    T = hb.shape[0]
    E = N_EXPERTS
    assert tb == RANK_SUB
    return pl.pallas_call(
        functools.partial(_gather_kernel, cp=cp),
        name="gather",
        grid_spec=pltpu.PrefetchScalarGridSpec(
            num_scalar_prefetch=2,
            grid=(T // tb,),
            in_specs=[
                pl.BlockSpec((tb, D_MODEL), lambda j, c, o: (j, 0)),
                pl.BlockSpec((tb, LANES), lambda j, c, o: (j, 0)),
                pl.BlockSpec((1, E, tb), lambda j, c, o: (j, 0, 0)),
            ],
            out_specs=pl.BlockSpec(memory_space=pl.ANY),
            scratch_shapes=[
                pltpu.VMEM((tb, XE_WIDTH), BF16),
                pltpu.VMEM((2, E * SLOT_ROWS, XE_WIDTH), BF16),
                pltpu.VMEM((SLOT_ROWS, XE_WIDTH), BF16),
                pltpu.VMEM((BF16_SUBLANES, XE_WIDTH), BF16),
                pltpu.SemaphoreType.DMA((2,)),
                pltpu.SemaphoreType.DMA((1,)),
            ],
        ),
        out_shape=jax.ShapeDtypeStruct((E, cp, XE_WIDTH), BF16),
        compiler_params=pltpu.CompilerParams(
            dimension_semantics=("arbitrary",), vmem_limit_bytes=VMEM_LIMIT),
    )(cnt_flat, off_flat, hb, g3, lr)


def _ffn_kernel(tot_s, x_ref, wg_ref, wu_ref, wd_ref, y_ref, acc_ref):
    e = pl.program_id(0)
    i = pl.program_id(1)
    tm = x_ref.shape[0]
    live = i * tm < tot_s[e]

    @pl.when(live)
    def _():
        x = x_ref[:, 0:D_MODEL]
        for c in range(EXPERT_FF // FFN_FCHUNK):
            cs = slice(c * FFN_FCHUNK, (c + 1) * FFN_FCHUNK)
            g = _dot(x, wg_ref[:, cs])
            u = _dot(x, wu_ref[:, cs])
            hid = (_silu(g) * u).astype(BF16)
            part = _dot(hid, wd_ref[cs, :])
            if c == 0:
                acc_ref[...] = part
            else:
                acc_ref[...] += part
        lane = lax.broadcasted_iota(I32, (1, LANES), 1)
        mine = (lane == e) | (lane == e + N_EXPERTS) | (lane == e + 2 * N_EXPERTS)
        pieces = x_ref[:, D_MODEL:XE_WIDTH].astype(F32)
        gate = jnp.sum(jnp.where(mine, pieces, 0.0), axis=1, keepdims=True)
        y_ref[...] = (acc_ref[...] * gate).astype(BF16)

    @pl.when(jnp.logical_not(live))
    def _():
        y_ref[...] = jnp.zeros(y_ref.shape, BF16)


def _ffn(tot, xe, wg, wu, wd):
    E, cp, _ = xe.shape
    tm = FFN_TILE
    return pl.pallas_call(
        _ffn_kernel,
        name="ffn",
        grid_spec=pltpu.PrefetchScalarGridSpec(
            num_scalar_prefetch=1,
            grid=(E, cp // tm),
            in_specs=[
                pl.BlockSpec((None, tm, XE_WIDTH), lambda e, i, t: (e, i, 0)),
                pl.BlockSpec((None, D_MODEL, EXPERT_FF), lambda e, i, t: (e, 0, 0)),
                pl.BlockSpec((None, D_MODEL, EXPERT_FF), lambda e, i, t: (e, 0, 0)),
                pl.BlockSpec((None, EXPERT_FF, D_MODEL), lambda e, i, t: (e, 0, 0)),
            ],
            out_specs=pl.BlockSpec((None, tm, D_MODEL), lambda e, i, t: (e, i, 0)),
            scratch_shapes=[pltpu.VMEM((tm, D_MODEL), F32)],
        ),
        out_shape=jax.ShapeDtypeStruct((E, cp, D_MODEL), BF16),
        compiler_params=pltpu.CompilerParams(
            dimension_semantics=("parallel", "arbitrary"), vmem_limit_bytes=VMEM_LIMIT),
    )(tot, xe, wg, wu, wd)


def _combine_kernel(cnt_s, off_s, ye_hbm, lrt_ref, h_ref, g2_ref, b2_ref, y_ref,
                    ybuf, obuf, sem, osem):
    j = pl.program_id(0)
    nb = pl.num_programs(0)
    E = N_EXPERTS
    RS = SLOT_ROWS
    slot = j % 2

    def fetch(jj, e, s):
        off = off_s[jj * E + e]
        return pltpu.make_async_copy(
            ye_hbm.at[e, pl.ds(pl.multiple_of(off, BF16_SUBLANES), RS)],
            ybuf.at[s, pl.ds(e * RS, RS)],
            sem.at[s])

    @pl.when(j == 0)
    def _():
        for e in range(E):
            fetch(0, e, 0).start()

    @pl.when(j + 1 < nb)
    def _():
        for e in range(E):
            fetch(j + 1, e, 1 - slot).start()

    for e in range(E):
        fetch(j, e, slot).wait()

    cols = lax.broadcasted_iota(I32, (1, RS), 1)

    def onehot(e, base):
        return jnp.where(lrt_ref[:, e:e + 1] == cols + base, 1.0, 0.0).astype(BF16)

    stacked = jnp.concatenate([onehot(e, 0) for e in range(E)], axis=1)
    y_ref[...] = _dot(stacked, ybuf[slot])

    for e in range(E):
        def extra(rc, carry, e=e):
            base = (rc + 1) * RS
            src = pl.multiple_of(off_s[j * E + e] + base, BF16_SUBLANES)
            cpy = pltpu.make_async_copy(ye_hbm.at[e, pl.ds(src, RS)], obuf, osem.at[0])
            cpy.start()
            cpy.wait()
            y_ref[...] += _dot(onehot(e, base), obuf[...])
            return carry
        lax.fori_loop(0, (cnt_s[j * E + e] + (RS - 1)) // RS - 1, extra, 0)

    y_ref[...] = _layer_norm(ALPHA * h_ref[...] + y_ref[...], g2_ref[...], b2_ref[...])


def _combine(cnt_flat, off_flat, ye, lrt, h, g2, b2, tb):
    T = h.shape[0]
    E = N_EXPERTS
    return pl.pallas_call(
        _combine_kernel,
        name="combine",
        grid_spec=pltpu.PrefetchScalarGridSpec(
            num_scalar_prefetch=2,
            grid=(T // tb,),
            in_specs=[
                pl.BlockSpec(memory_space=pl.ANY),
                pl.BlockSpec((tb, E), lambda j, c, o: (j, 0)),
                pl.BlockSpec((tb, D_MODEL), lambda j, c, o: (j, 0)),
                pl.BlockSpec((1, D_MODEL), lambda j, c, o: (0, 0)),
                pl.BlockSpec((1, D_MODEL), lambda j, c, o: (0, 0)),
            ],
            out_specs=pl.BlockSpec((tb, D_MODEL), lambda j, c, o: (j, 0)),
            scratch_shapes=[
                pltpu.VMEM((2, E * SLOT_ROWS, D_MODEL), BF16),
                pltpu.VMEM((SLOT_ROWS, D_MODEL), BF16),
                pltpu.SemaphoreType.DMA((2,)),
                pltpu.SemaphoreType.DMA((1,)),
            ],
        ),
        out_shape=jax.ShapeDtypeStruct((T, D_MODEL), F32),
        compiler_params=pltpu.CompilerParams(
            dimension_semantics=("arbitrary",), vmem_limit_bytes=VMEM_LIMIT),
    )(cnt_flat, off_flat, ye, lrt, h, g2, b2)


def _rope_tables(seq_len):
    half = ROT_DIM // 2
    inv_freq = ROPE_THETA ** (-jnp.arange(0, ROT_DIM, 2, dtype=F32) / ROT_DIM)
    ang = jnp.arange(seq_len, dtype=F32)[:, None] * inv_freq[None, :]
    cos = jnp.cos(ang)
    sin = jnp.sin(ang)
    ones = jnp.ones((seq_len, DIFF_DH - ROT_DIM), F32)
    zeros = jnp.zeros((seq_len, DIFF_DH - ROT_DIM), F32)
    zh = jnp.zeros((seq_len, half), F32)
    ra = jnp.concatenate([cos, cos, ones], axis=1)
    rb = jnp.concatenate([zh, sin, zeros], axis=1)
    rc = jnp.concatenate([-sin, zh, zeros], axis=1)
    two = lambda t: jnp.concatenate([t, t], axis=1)
    return two(ra), two(rb), two(rc)


def _moe_tiles(T):
    tb = RANK_SUB
    cap = (CAPACITY_FACTOR * T) // N_EXPERTS
    nblk = T // tb
    cp = cap + BF16_SUBLANES * nblk + SLOT_ROWS
    cp = -(-cp // FFN_TILE) * FFN_TILE
    return tb, cap, cp


def _attn_tiles(S):
    return min(1024, S), min(1024, S), min(512, S)


def _values_transposed(dv, B, S):
    vt = dv.reshape(B, S, DIFF_HEADS, DIFF_DV).transpose(0, 2, 3, 1)
    ones = jnp.ones((B, DIFF_HEADS, VT_ROWS - DIFF_DV, S), BF16)
    return jnp.concatenate([vt, ones], axis=2).reshape(B, DIFF_HEADS * VT_ROWS, S)


def _encoder_group(x, wts):
    B, S, D = x.shape
    T = B * S
    x2 = x.reshape(T, D)
    ra, rb, rc = _rope_tables(S)
    qkv, gg, la, dq, dk, dv = _inproj(x2, wts["w_all"], wts["wlr"], wts["blr"], ra, rb, rc, S)
    o_f, o_b = _gla(qkv.reshape(B, S, 1024), la.reshape(B, S, 2 * GLA_QK))
    qb, kb, qc = _attn_tiles(S)
    md = _attn(wts["lam4"], wts["diff_g"], dq.reshape(B, S, DIFF_WIDTH),
               dk.reshape(B, S, DIFF_WIDTH), _values_transposed(dv, B, S), qb, kb, qc)
    h, hb, g3, aff_blocks = _post(
        o_f.reshape(T, GLA_WIDTH), o_b.reshape(T, GLA_WIDTH), gg, md.reshape(T, DIFF_WIDTH), x2,
        wts["gla_g"], wts["w_o"], wts["ln1_g"], wts["ln1_b"], wts["w_router"], wts["w_router_t"])
    tb, cap, cp = _moe_tiles(T)
    lr, cnt, off, tot = _route(aff_blocks, cap, tb)
    cnt_flat = cnt[:, :, 0].reshape(-1)
    off_flat = off[:, :, 0].reshape(-1)
    lrt = lr.transpose(0, 2, 1).reshape(T, N_EXPERTS)
    xe = _gather(cnt_flat, off_flat, hb, g3, lr, cp, tb)
    ye = _ffn(tot[:, 0], xe, wts["w_gate"], wts["w_up"], wts["w_down"])
    y = _combine(cnt_flat, off_flat, ye, lrt, h, wts["ln2_g"], wts["ln2_b"], tb)
    return y.reshape(B, S, D)


def _prep_weights(w_in, w_lr_f, b_lr_f, w_lr_b, b_lr_b, gla_norm_g,
                  lambda_q1, lambda_k1, lambda_q2, lambda_k2, diff_norm_g, w_o,
                  ln1_g, ln1_b, w_router, w_gate, w_up, w_down, ln2_g, ln2_b):
    sizes = (GLA_QK, GLA_QK, GLA_WIDTH, GLA_WIDTH, GLA_RANK, GLA_RANK,
             DIFF_WIDTH, DIFF_WIDTH, DIFF_WIDTH)
    cols = []
    start = 0
    for s in sizes:
        cols.append(w_in[0][:, start:start + s])
        start += s
    gq, gk, gv, gg, zf, zb, dq, dk, dv = cols
    zpad = jnp.zeros((D_MODEL, LANES - 2 * GLA_RANK), F32)
    w_all = jnp.concatenate([gq, gk, gv, gg, zf, zb, zpad, dq, dk, dv], axis=1).astype(BF16)
    wlr = jnp.zeros((LANES, 2 * GLA_QK), F32)
    wlr = wlr.at[0:GLA_RANK, 0:GLA_QK].set(w_lr_f[0])
    wlr = wlr.at[GLA_RANK:2 * GLA_RANK, GLA_QK:].set(w_lr_b[0])
    return dict(
        w_all=w_all,
        wlr=wlr.astype(BF16),
        blr=jnp.concatenate([b_lr_f[0], b_lr_b[0]])[None, :],
        gla_g=gla_norm_g[0][None, :],
        lam4=jnp.stack([lambda_q1[0], lambda_k1[0], lambda_q2[0], lambda_k2[0]]),
        diff_g=diff_norm_g[0][None, :],
        w_o=w_o[0].astype(BF16),
        ln1_g=ln1_g[0][None, :], ln1_b=ln1_b[0][None, :],
        w_router=w_router[0], w_router_t=w_router[0].T,
        w_gate=w_gate[0].astype(BF16), w_up=w_up[0].astype(BF16), w_down=w_down[0].astype(BF16),
        ln2_g=ln2_g[0][None, :], ln2_b=ln2_b[0][None, :],
    )


def kernel(x_prompt, x_sample, w_in, w_lr_f, b_lr_f, w_lr_b, b_lr_b, gla_norm_g,
           lambda_q1, lambda_k1, lambda_q2, lambda_k2, diff_norm_g, w_o,
           ln1_g, ln1_b, w_router, w_gate, w_up, w_down, ln2_g, ln2_b):
    wts = _prep_weights(w_in, w_lr_f, b_lr_f, w_lr_b, b_lr_b, gla_norm_g,
                        lambda_q1, lambda_k1, lambda_q2, lambda_k2, diff_norm_g, w_o,
                        ln1_g, ln1_b, w_router, w_gate, w_up, w_down, ln2_g, ln2_b)
    return (_encoder_group(x_prompt, wts), _encoder_group(x_sample, wts))
```

```python
import functools
import math

import jax
import jax.numpy as jnp
import numpy as np
from jax import lax
from jax.experimental import pallas as pl
from jax.experimental.pallas import tpu as pltpu

F32 = jnp.float32
BF16 = jnp.bfloat16
I32 = jnp.int32

D_MODEL = 1024
GLA_HEADS = 4
GLA_DK = 64
GLA_DV = 128
GLA_RANK = 16
GLA_GATE_TAU = 16.0
GLA_CHUNK = 64
GLA_QK = GLA_HEADS * GLA_DK
GLA_WIDTH = GLA_HEADS * GLA_DV
DIFF_HEADS = 4
DIFF_DV = 128
DIFF_DH = 64
DIFF_WIDTH = DIFF_HEADS * DIFF_DV
ROT_DIM = DIFF_DH // 4
ROPE_THETA = 500000.0
N_EXPERTS = 16
CAPACITY_FACTOR = 2
EXPERT_FF = 2816
DEPTH = 1
ALPHA = (2 * DEPTH) ** 0.25
EPS = 1e-5
LAMBDA_INIT = 0.8 - 0.6 * math.exp(-0.3 * 0)

LANES = 128
BF16_SUBLANES = 16
VMEM_LIMIT = 56 * 1024 * 1024

ROW_TILE = 512
RANK_SUB = 512
GLA_BLOCK = 512
SLOT_ROWS = 128
XE_WIDTH = D_MODEL + LANES
NOT_SELECTED = -(1 << 20)
FFN_TILE = 512
FFN_FCHUNK = 256
VT_ROWS = DIFF_DV + BF16_SUBLANES
LOG2E = math.log2(math.e)

C_QKV = 0
C_GG = 1024
C_Z = 1536
C_DQK = 1664
C_DV = 2688
W_IN_COLS = 3200


def _nt_dot(a, b):
    return lax.dot_general(a, b, (((1,), (1,)), ((), ())), preferred_element_type=F32)


def _tn_dot(a, b):
    return lax.dot_general(a, b, (((0,), (0,)), ((), ())), preferred_element_type=F32)


def _dot(a, b):
    return jnp.dot(a, b, preferred_element_type=F32)


def _split3(x):
    hi = x.astype(BF16)
    r1 = x - hi.astype(F32)
    md = r1.astype(BF16)
    lo = (r1 - md.astype(F32)).astype(BF16)
    return hi, md, lo


def _layer_norm(v, g, b):
    mu = jnp.mean(v, axis=-1, keepdims=True)
    c = v - mu
    var = jnp.mean(c * c, axis=-1, keepdims=True)
    return c * lax.rsqrt(var + EPS) * g + b


def _silu(v):
    return v / (1.0 + jnp.exp(-v))


def _inproj_kernel(x_ref, w_ref, wlr_ref, blr_ref, ra_ref, rb_ref, rc_ref,
                   qkv_ref, gg_ref, la_ref, dq_ref, dk_ref, dv_ref):
    xb = x_ref[...].astype(BF16)

    def mm(lo, width):
        return _dot(xb, w_ref[:, lo:lo + width])

    g = mm(C_QKV, 1024)
    qkv_ref[:, 0:GLA_QK] = g[:, 0:GLA_QK] * (GLA_DK ** -0.5)
    qkv_ref[:, GLA_QK:1024] = g[:, GLA_QK:1024]
    gg_ref[...] = mm(C_GG, GLA_WIDTH)

    z = mm(C_Z, LANES).astype(BF16)
    pre = _dot(z, wlr_ref[...]) + blr_ref[...]
    log_sig = jnp.minimum(pre, 0.0) - jnp.log(1.0 + jnp.exp(-jnp.abs(pre)))
    la_ref[...] = log_sig * (1.0 / GLA_GATE_TAU)

    d = mm(C_DQK, 2 * DIFF_WIDTH)
    ra = ra_ref[...]
    rb = rb_ref[...]
    rc = rc_ref[...]
    for hh in range(2 * DIFF_HEADS):
        t = d[:, hh * LANES:(hh + 1) * LANES]
        y = t * ra + pltpu.roll(t, ROT_DIM // 2, 1) * rb + pltpu.roll(t, LANES - ROT_DIM // 2, 1) * rc
        if hh < DIFF_HEADS:
            dq_ref[:, hh * LANES:(hh + 1) * LANES] = (y * (LOG2E * DIFF_DH ** -0.5)).astype(BF16)
        else:
            h2 = hh - DIFF_HEADS
            dk_ref[:, h2 * LANES:(h2 + 1) * LANES] = y.astype(BF16)
    dv_ref[...] = mm(C_DV, DIFF_WIDTH).astype(BF16)


def _inproj(x2, w_all, wlr, blr, ra, rb, rc, seq_len):
    T = x2.shape[0]
    tm = ROW_TILE
    sblocks = seq_len // tm
    row = lambda i: (i, 0)
    const = lambda i: (0, 0)
    rope = lambda i: (i % sblocks, 0)
    return pl.pallas_call(
        _inproj_kernel,
        name="inproj",
        grid=(T // tm,),
        in_specs=[
            pl.BlockSpec((tm, D_MODEL), row),
            pl.BlockSpec((D_MODEL, W_IN_COLS), const),
            pl.BlockSpec((LANES, 2 * GLA_QK), const),
            pl.BlockSpec((1, 2 * GLA_QK), const),
            pl.BlockSpec((tm, LANES), rope),
            pl.BlockSpec((tm, LANES), rope),
            pl.BlockSpec((tm, LANES), rope),
        ],
        out_specs=[
            pl.BlockSpec((tm, 1024), row),
            pl.BlockSpec((tm, GLA_WIDTH), row),
            pl.BlockSpec((tm, 2 * GLA_QK), row),
            pl.BlockSpec((tm, DIFF_WIDTH), row),
            pl.BlockSpec((tm, DIFF_WIDTH), row),
            pl.BlockSpec((tm, DIFF_WIDTH), row),
        ],
        out_shape=[
            jax.ShapeDtypeStruct((T, 1024), F32),
            jax.ShapeDtypeStruct((T, GLA_WIDTH), F32),
            jax.ShapeDtypeStruct((T, 2 * GLA_QK), F32),
            jax.ShapeDtypeStruct((T, DIFF_WIDTH), BF16),
            jax.ShapeDtypeStruct((T, DIFF_WIDTH), BF16),
            jax.ShapeDtypeStruct((T, DIFF_WIDTH), BF16),
        ],
        compiler_params=pltpu.CompilerParams(
            dimension_semantics=("parallel",), vmem_limit_bytes=VMEM_LIMIT),
    )(x2, w_all, wlr, blr, ra, rb, rc)


def _gla_kernel(qkvf_ref, laf_ref, qkvb_ref, lab_ref, of_ref, ob_ref, sf_ref, sb_ref, *, nsub):
    @pl.when(pl.program_id(1) == 0)
    def _():
        sf_ref[...] = jnp.zeros(sf_ref.shape, F32)
        sb_ref[...] = jnp.zeros(sb_ref.shape, F32)

    L = GLA_CHUNK
    row = lax.broadcasted_iota(I32, (L, L), 0)
    col = lax.broadcasted_iota(I32, (L, L), 1)
    lower = row >= col
    upper = row <= col
    lane = lax.broadcasted_iota(I32, (1, GLA_QK), 1)
    hmask = [(lane >= h * GLA_DK) & (lane < (h + 1) * GLA_DK) for h in range(GLA_HEADS)]

    def chunk(qkv_ref, la_ref, o_ref, s_ref, c, keep, mid_row, end_row):
        r0 = c * L
        la = la_ref[r0:r0 + L, :]
        tri = jnp.where(keep, 1.0, 0.0).astype(BF16)
        hi, md, lo = _split3(la)
        b = _dot(tri, hi) + _dot(tri, md) + _dot(tri, lo)
        yield
        q = qkv_ref[r0:r0 + L, 0:GLA_QK]
        k = qkv_ref[r0:r0 + L, GLA_QK:2 * GLA_QK]
        bm = b[mid_row:mid_row + 1, :]
        be = b[end_row:end_row + 1, :]
        qs = q * jnp.exp(b - bm)
        ks = (k * jnp.exp(bm - b)).astype(BF16)
        kd = (k * jnp.exp(be - b)).astype(BF16)
        qe = q * jnp.exp(b)
        dec = jnp.exp(be)
        qstack = jnp.concatenate([jnp.where(hmask[h], qs, 0.0) for h in range(GLA_HEADS)],
                                 axis=0).astype(BF16)
        sc = _nt_dot(qstack, ks)
        yield
        vb = qkv_ref[r0:r0 + L, 2 * GLA_QK:1024].astype(BF16)
        for h in range(GLA_HEADS):
            s_h = jnp.where(keep, sc[h * L:(h + 1) * L, :], 0.0).astype(BF16)
            v_h = vb[:, h * GLA_DV:(h + 1) * GLA_DV]
            st = s_ref[h]
            qe_h = jnp.where(hmask[h], qe, 0.0).astype(BF16)
            o_ref[r0:r0 + L, h * GLA_DV:(h + 1) * GLA_DV] = (
                _dot(s_h, v_h) + _nt_dot(qe_h, st.astype(BF16)))
            s_ref[h] = st * dec + _tn_dot(v_h, kd)
        yield

    items = []
    for c in range(nsub):
        items.append(chunk(qkvf_ref, laf_ref, of_ref, sf_ref, c, lower, L // 2 - 1, L - 1))
        items.append(chunk(qkvb_ref, lab_ref, ob_ref, sb_ref, nsub - 1 - c, upper, L // 2, 0))
    n = len(items)
    for i in range(n + 2):
        for stage in range(3):
            if 0 <= i - stage < n:
                next(items[i - stage])


def _gla(qkv, la):
    B, S, _ = qkv.shape
    gb = GLA_BLOCK
    n = S // gb
    fwd = lambda b, i: (b, i, 0)
    bwd = lambda b, i: (b, n - 1 - i, 0)
    bwd_la = lambda b, i: (b, n - 1 - i, 1)
    return pl.pallas_call(
        functools.partial(_gla_kernel, nsub=gb // GLA_CHUNK),
        name="gla",
        grid=(B, n),
        in_specs=[
            pl.BlockSpec((None, gb, 1024), fwd),
            pl.BlockSpec((None, gb, GLA_QK), fwd),
            pl.BlockSpec((None, gb, 1024), bwd),
            pl.BlockSpec((None, gb, GLA_QK), bwd_la),
        ],
        out_specs=[
            pl.BlockSpec((None, gb, GLA_WIDTH), fwd),
            pl.BlockSpec((None, gb, GLA_WIDTH), bwd),
        ],
        out_shape=[jax.ShapeDtypeStruct((B, S, GLA_WIDTH), F32)] * 2,
        scratch_shapes=[pltpu.VMEM((GLA_HEADS, GLA_DV, GLA_QK), F32)] * 2,
        compiler_params=pltpu.CompilerParams(
            dimension_semantics=("parallel", "arbitrary"), vmem_limit_bytes=VMEM_LIMIT),
    )(qkv, la, qkv, la)


def _attn_kernel(lam_ref, g_ref, q_ref, k_ref, vt_ref, o_ref,
                 q1_ref, q2_ref, m1_ref, a1_ref, m2_ref, a2_ref, *, qc):
    ki = pl.program_id(3)
    qb = q_ref.shape[0]

    @pl.when(ki == 0)
    def _():
        q = q_ref[...]
        lane = lax.broadcasted_iota(I32, (1, LANES), 1)
        zero = jnp.zeros_like(q)
        q1_ref[...] = jnp.where(lane < DIFF_DH, q, zero)
        q2_ref[...] = jnp.where(lane >= DIFF_DH, q, zero)
        for m_ref, a_ref in ((m1_ref, a1_ref), (m2_ref, a2_ref)):
            m_ref[...] = jnp.full(m_ref.shape, -jnp.inf, F32)
            a_ref[...] = jnp.zeros(a_ref.shape, F32)

    k = k_ref[...]
    vt = vt_ref[...]
    units = [(c, refs) for c in range(qb // qc)
             for refs in ((q1_ref, m1_ref, a1_ref), (q2_ref, m2_ref, a2_ref))]
    live = {}

    def scores(i):
        c, (qm_ref, m_ref, _) = units[i]
        cs = slice(c * qc, (c + 1) * qc)
        s = _nt_dot(k, qm_ref[cs, :])
        m_old = m_ref[:, cs]
        m_new = jnp.maximum(m_old, jnp.max(s, axis=0, keepdims=True))
        m_ref[:, cs] = m_new
        live[i] = (s, m_new, jnp.exp2(m_old - m_new))

    def probs(i):
        s, m_new, corr = live[i]
        live[i] = (jnp.exp2(s - m_new).astype(BF16), corr)

    def values(i):
        c, (_, _, a_ref) = units[i]
        cs = slice(c * qc, (c + 1) * qc)
        p, corr = live.pop(i)
        a_ref[:, cs] = corr * a_ref[:, cs] + _dot(vt, p)

    n = len(units)
    for i in range(n + 2):
        if i < n:
            scores(i)
        if 0 <= i - 1 < n:
            probs(i - 1)
        if 0 <= i - 2 < n:
            values(i - 2)

    @pl.when(ki == pl.num_programs(3) - 1)
    def _():
        lp = lam_ref[...]
        lam = (jnp.exp(jnp.sum(lp[0:1] * lp[1:2], axis=1, keepdims=True))
               - jnp.exp(jnp.sum(lp[2:3] * lp[3:4], axis=1, keepdims=True)) + LAMBDA_INIT)
        o = (a1_ref[0:DIFF_DV, :] / a1_ref[DIFF_DV:DIFF_DV + 1, :]
             - lam * (a2_ref[0:DIFF_DV, :] / a2_ref[DIFF_DV:DIFF_DV + 1, :]))
        ot = o.T
        r = ot * lax.rsqrt(jnp.mean(ot * ot, axis=1, keepdims=True) + EPS) * g_ref[...]
        o_ref[...] = (r * (1.0 - LAMBDA_INIT)).astype(BF16)


def _attn(lam4, g, dq, dk, dvt, qb, kb, qc):
    B, S, _ = dq.shape
    qmap = lambda b, h, qi, ki: (b, qi, h)
    kmap = lambda b, h, qi, ki: (b, ki, h)
    vmap = lambda b, h, qi, ki: (b, h, ki)
    const = lambda b, h, qi, ki: (0, 0)
    return pl.pallas_call(
        functools.partial(_attn_kernel, qc=qc),
        name="attn",
        grid=(B, DIFF_HEADS, S // qb, S // kb),
        in_specs=[
            pl.BlockSpec((4, DIFF_DH), const),
            pl.BlockSpec((1, DIFF_DV), const),
            pl.BlockSpec((None, qb, LANES), qmap),
            pl.BlockSpec((None, kb, LANES), kmap),
            pl.BlockSpec((None, VT_ROWS, kb), vmap),
        ],
        out_specs=pl.BlockSpec((None, qb, LANES), qmap),
        out_shape=jax.ShapeDtypeStruct((B, S, DIFF_WIDTH), BF16),
        scratch_shapes=[
            pltpu.VMEM((qb, LANES), BF16), pltpu.VMEM((qb, LANES), BF16),
            pltpu.VMEM((1, qb), F32), pltpu.VMEM((VT_ROWS, qb), F32),
            pltpu.VMEM((1, qb), F32), pltpu.VMEM((VT_ROWS, qb), F32),
        ],
        compiler_params=pltpu.CompilerParams(
            dimension_semantics=("parallel", "parallel", "parallel", "arbitrary"),
            vmem_limit_bytes=VMEM_LIMIT),
    )(lam4, g, dq, dk, dvt)


def _post_kernel(of_ref, ob_ref, gg_ref, md_ref, x_ref, gn_ref, wo_ref, g1_ref, b1_ref,
                 wr_ref, wrt_ref, h_ref, hb_ref, g3_ref, afft_ref):
    gn = gn_ref[...]
    parts = []
    for h in range(GLA_HEADS):
        sl = slice(h * GLA_DV, (h + 1) * GLA_DV)
        o = of_ref[:, sl] + ob_ref[:, sl]
        r = o * lax.rsqrt(jnp.mean(o * o, axis=1, keepdims=True) + EPS) * gn
        parts.append((r * _silu(gg_ref[:, sl])).astype(BF16))
    mixed_gla = jnp.concatenate(parts, axis=1)
    mix = _dot(mixed_gla, wo_ref[0:GLA_WIDTH, :]) + _dot(md_ref[...], wo_ref[GLA_WIDTH:, :])
    hval = _layer_norm(ALPHA * x_ref[...] + mix, g1_ref[...], b1_ref[...])
    h_ref[...] = hval
    hh = hval.astype(BF16)
    hb_ref[...] = hh
    hl = (hval - hh.astype(F32)).astype(BF16)
    wr = wr_ref[...]
    wh = wr.astype(BF16)
    wl = (wr - wh.astype(F32)).astype(BF16)
    logits = _dot(hh, wh) + _dot(hl, wh) + _dot(hh, wl)
    e = jnp.exp(logits - jnp.max(logits, axis=1, keepdims=True))
    aff = e / jnp.sum(e, axis=1, keepdims=True)
    a_hi, a_md, a_lo = _split3(aff)
    pad = jnp.zeros((aff.shape[0], LANES - 3 * N_EXPERTS), F32)
    g3_ref[...] = jnp.concatenate(
        [a_hi.astype(F32), a_md.astype(F32), a_lo.astype(F32), pad], axis=1).astype(BF16)
    wrt = wrt_ref[...]
    wth = wrt.astype(BF16)
    wtl = (wrt - wth.astype(F32)).astype(BF16)
    lt = _nt_dot(wth, hh) + _nt_dot(wth, hl) + _nt_dot(wtl, hh)
    et = jnp.exp(lt - jnp.max(lt, axis=0, keepdims=True))
    afft_ref[...] = et / jnp.sum(et, axis=0, keepdims=True)


def _post(o_f, o_b, gg, md, x2, gn, wo, g1, b1, wr, wrt):
    T = x2.shape[0]
    tm = ROW_TILE
    assert tm == RANK_SUB
    row = lambda i: (i, 0)
    const = lambda i: (0, 0)
    return pl.pallas_call(
        _post_kernel,
        name="post",
        grid=(T // tm,),
        in_specs=[
            pl.BlockSpec((tm, GLA_WIDTH), row),
            pl.BlockSpec((tm, GLA_WIDTH), row),
            pl.BlockSpec((tm, GLA_WIDTH), row),
            pl.BlockSpec((tm, DIFF_WIDTH), row),
            pl.BlockSpec((tm, D_MODEL), row),
            pl.BlockSpec((1, GLA_DV), const),
            pl.BlockSpec((D_MODEL, D_MODEL), const),
            pl.BlockSpec((1, D_MODEL), const),
            pl.BlockSpec((1, D_MODEL), const),
            pl.BlockSpec((D_MODEL, N_EXPERTS), const),
            pl.BlockSpec((N_EXPERTS, D_MODEL), const),
        ],
        out_specs=[
            pl.BlockSpec((tm, D_MODEL), row),
            pl.BlockSpec((tm, D_MODEL), row),
            pl.BlockSpec((tm, LANES), row),
            pl.BlockSpec((None, N_EXPERTS, tm), lambda i: (i, 0, 0)),
        ],
        out_shape=[
            jax.ShapeDtypeStruct((T, D_MODEL), F32),
            jax.ShapeDtypeStruct((T, D_MODEL), BF16),
            jax.ShapeDtypeStruct((T, LANES), BF16),
            jax.ShapeDtypeStruct((T // tm, N_EXPERTS, tm), F32),
        ],
        compiler_params=pltpu.CompilerParams(
            dimension_semantics=("parallel",), vmem_limit_bytes=VMEM_LIMIT),
    )(o_f, o_b, gg, md, x2, gn, wo, g1, b1, wr, wrt)


def _route_kernel(aff_ref, lr_ref, cnt_ref, off_ref, *, cap, nblk, spb):
    E = N_EXPERTS

    def count(mask):
        part = jnp.sum(jnp.where(mask, 1, 0).astype(I32), axis=0)
        return jnp.sum(part, axis=1, keepdims=True)

    def search(i, thr_bits):
        cand = thr_bits | jnp.left_shift(jnp.int32(1), 30 - i)
        cand_f = pltpu.bitcast(cand, F32)
        return jnp.where(count(aff_ref[...] >= cand_f[None]) >= cap, cand, thr_bits)

    thr = pltpu.bitcast(lax.fori_loop(0, 31, search, jnp.zeros((E, 1), I32)), F32)
    need = cap - count(aff_ref[...] > thr[None])

    r = lax.broadcasted_iota(I32, (RANK_SUB, RANK_SUB), 0)
    c = lax.broadcasted_iota(I32, (RANK_SUB, RANK_SUB), 1)
    before = jnp.where(r < c, 1.0, 0.0).astype(BF16)

    def block(j, carry):
        eqc, offc = carry
        selc = jnp.zeros((E, 1), I32)
        for s in range(spb):
            idx = j * spb + s
            a = aff_ref[idx]
            gt = a > thr
            eq = a == thr
            eqrank = _dot(jnp.where(eq, 1.0, 0.0).astype(BF16), before).astype(I32) + eqc
            sel = gt | (eq & (eqrank < need))
            rank = _dot(jnp.where(sel, 1.0, 0.0).astype(BF16), before).astype(I32) + selc
            lr_ref[idx] = jnp.where(sel, rank, NOT_SELECTED)
            eqc = eqc + jnp.sum(jnp.where(eq, 1, 0).astype(I32), axis=1, keepdims=True)
            selc = selc + jnp.sum(jnp.where(sel, 1, 0).astype(I32), axis=1, keepdims=True)
        cnt_ref[j] = jnp.broadcast_to(selc, (E, LANES))
        off_ref[j] = jnp.broadcast_to(offc, (E, LANES))
        return eqc, offc + selc

    zero = jnp.zeros((E, 1), I32)
    lax.fori_loop(0, nblk, block, (zero, zero))


def _route(aff_blocks, cap, tb):
    nsb = aff_blocks.shape[0]
    spb = tb // RANK_SUB
    nblk = nsb // spb
    E = N_EXPERTS
    return pl.pallas_call(
        functools.partial(_route_kernel, cap=cap, nblk=nblk, spb=spb),
        name="route",
        out_shape=[
            jax.ShapeDtypeStruct((nsb, E, RANK_SUB), I32),
            jax.ShapeDtypeStruct((nblk, E, LANES), I32),
            jax.ShapeDtypeStruct((nblk, E, LANES), I32),
        ],
        compiler_params=pltpu.CompilerParams(vmem_limit_bytes=VMEM_LIMIT),
    )(aff_blocks)


def _gather_kernel(cnt_s, off_s, h_ref, g3_ref, lr_ref, xe_out, haug, xbuf, obuf, zbuf, tail, sem,
                   osem, *, cp):
    j = pl.program_id(0)
    nb = pl.num_programs(0)
    E = N_EXPERTS
    RS = SLOT_ROWS
    TR = BF16_SUBLANES
    slot = j % 2

    @pl.when(j == 0)
    def _():
        tail[...] = jnp.zeros(tail.shape, BF16)

    haug[:, 0:D_MODEL] = h_ref[...]
    haug[:, D_MODEL:XE_WIDTH] = g3_ref[...]
    rows = lax.broadcasted_iota(I32, (RS, 1), 0)

    def fill(jj, e):
        return off_s[jj * E + e]

    def onehot(e, base):
        pos = lr_ref[0, e:e + 1, :] + (fill(j, e) % TR)
        return jnp.where(pos == rows + base, 1.0, 0.0).astype(BF16)

    half = E // 2
    for g in range(2):
        stacked = jnp.concatenate([onehot(e, 0) for e in range(g * half, (g + 1) * half)], axis=0)
        xbuf[slot, g * half * RS:(g + 1) * half * RS, :] = _dot(stacked, haug[...]).astype(BF16)

    def copy(jj, e, s):
        start = fill(jj, e) // TR * TR
        return pltpu.make_async_copy(
            xbuf.at[s, pl.ds(e * RS, RS)],
            xe_out.at[e, pl.ds(pl.multiple_of(start, TR), RS)],
            sem.at[s])

    def chunks(e):
        return (fill(j, e) % TR + cnt_s[j * E + e] + (RS - 1)) // RS

    for e in range(E):
        r0 = e * RS
        xbuf[slot, r0:r0 + TR, :] = (xbuf[slot, r0:r0 + TR, :].astype(F32)
                                      + tail[e].astype(F32)).astype(BF16)
        end = fill(j, e) % TR + cnt_s[j * E + e]
        keep_at = end // TR * TR
        in_first = jnp.logical_and(keep_at < RS, end % TR > 0)
        src = pl.multiple_of(jnp.minimum(keep_at, RS - TR), TR)
        part = xbuf[slot, pl.ds(r0 + src, TR), :]
        tail[e] = jnp.where(in_first, part, jnp.zeros_like(part))

    @pl.when(j > 0)
    def _():
        for e in range(E):
            copy(j - 1, e, 1 - slot).wait()

    for e in range(E):
        copy(j, e, slot).start()

    for e in range(E):
        def extra(rc, carry, e=e):
            base = (rc + 1) * RS
            obuf[...] = _dot(onehot(e, base), haug[...]).astype(BF16)
            end = fill(j, e) % TR + cnt_s[j * E + e]
            keep_at = end // TR * TR

            @pl.when(jnp.logical_and(keep_at // RS == rc + 1, end % TR > 0))
            def _():
                tail[e] = obuf[pl.ds(pl.multiple_of(keep_at - base, TR), TR), :]

            dst = pl.multiple_of(fill(j, e) // TR * TR + base, TR)
            cpy = pltpu.make_async_copy(obuf, xe_out.at[e, pl.ds(dst, RS)], osem.at[0])
            cpy.start()
            cpy.wait()
            return carry
        lax.fori_loop(0, chunks(e) - 1, extra, 0)

    @pl.when(j == nb - 1)
    def _():
        for e in range(E):
            copy(j, e, slot).wait()
        zbuf[...] = jnp.zeros(zbuf.shape, BF16)
        ZR = BF16_SUBLANES

        def zcopy(e, z0, i):
            return pltpu.make_async_copy(
                zbuf, xe_out.at[e, pl.ds(pl.multiple_of(z0 + i * ZR, ZR), ZR)], osem.at[0])

        for e in range(E):
            z0 = fill(j, e) // TR * TR + jnp.maximum(chunks(e), 1) * RS
            n = (cp - z0) // ZR

            def zs(i, carry, e=e, z0=z0):
                zcopy(e, z0, i).start()
                return carry

            def zw(i, carry, e=e, z0=z0):
                zcopy(e, z0, i).wait()
                return carry
            lax.fori_loop(0, n, zs, 0)
            lax.fori_loop(0, n, zw, 0)


def _gather(cnt_flat, off_flat, hb, g3, lr, cp, tb):
    T = hb.shape[0]
    E = N_EXPERTS
    assert tb == RANK_SUB
    return pl.pallas_call(
        functools.partial(_gather_kernel, cp=cp),
        name="gather",
        grid_spec=pltpu.PrefetchScalarGridSpec(
            num_scalar_prefetch=2,
            grid=(T // tb,),
            in_specs=[
                pl.BlockSpec((tb, D_MODEL), lambda j, c, o: (j, 0)),
                pl.BlockSpec((tb, LANES), lambda j, c, o: (j, 0)),
                pl.BlockSpec((1, E, tb), lambda j, c, o: (j, 0, 0)),
            ],
            out_specs=pl.BlockSpec(memory_space=pl.ANY),
            scratch_shapes=[
                pltpu.VMEM((tb, XE_WIDTH), BF16),
                pltpu.VMEM((2, E * SLOT_ROWS, XE_WIDTH), BF16),
                pltpu.VMEM((SLOT_ROWS, XE_WIDTH), BF16),
                pltpu.VMEM((BF16_SUBLANES, XE_WIDTH), BF16),
                pltpu.VMEM((E, BF16_SUBLANES, XE_WIDTH), BF16),
                pltpu.SemaphoreType.DMA((2,)),
                pltpu.SemaphoreType.DMA((1,)),
            ],
        ),
        out_shape=jax.ShapeDtypeStruct((E, cp, XE_WIDTH), BF16),
        compiler_params=pltpu.CompilerParams(
            dimension_semantics=("arbitrary",), vmem_limit_bytes=VMEM_LIMIT),
    )(cnt_flat, off_flat, hb, g3, lr)


def _ffn_kernel(x_ref, wg_ref, wu_ref, wd_ref, y_ref, acc_ref):
    e = pl.program_id(0)
    i = pl.program_id(1)
    live = i < pl.num_programs(1) - 1

    @pl.when(live)
    def _():
        x = x_ref[:, 0:D_MODEL]
        for c in range(EXPERT_FF // FFN_FCHUNK):
            cs = slice(c * FFN_FCHUNK, (c + 1) * FFN_FCHUNK)
            g = _dot(x, wg_ref[:, cs])
            u = _dot(x, wu_ref[:, cs])
            hid = (_silu(g) * u).astype(BF16)
            part = _dot(hid, wd_ref[cs, :])
            if c == 0:
                acc_ref[...] = part
            else:
                acc_ref[...] += part
        lane = lax.broadcasted_iota(I32, (1, LANES), 1)
        mine = (lane == e) | (lane == e + N_EXPERTS) | (lane == e + 2 * N_EXPERTS)
        pieces = x_ref[:, D_MODEL:XE_WIDTH].astype(F32)
        gate = jnp.sum(jnp.where(mine, pieces, 0.0), axis=1, keepdims=True)
        y_ref[...] = (acc_ref[...] * gate).astype(BF16)

    @pl.when(jnp.logical_not(live))
    def _():
        y_ref[...] = jnp.zeros(y_ref.shape, BF16)


def _ffn(xe, wg, wu, wd, cap):
    E = xe.shape[0]
    tm = FFN_TILE
    nt = cap // tm
    return pl.pallas_call(
        _ffn_kernel,
        name="ffn",
        grid=(E, nt + 1),
        in_specs=[
            pl.BlockSpec((None, tm, XE_WIDTH), lambda e, i: (e, jnp.minimum(i, nt - 1), 0)),
            pl.BlockSpec((None, D_MODEL, EXPERT_FF), lambda e, i: (e, 0, 0)),
            pl.BlockSpec((None, D_MODEL, EXPERT_FF), lambda e, i: (e, 0, 0)),
            pl.BlockSpec((None, EXPERT_FF, D_MODEL), lambda e, i: (e, 0, 0)),
        ],
        out_specs=pl.BlockSpec((None, tm, D_MODEL), lambda e, i: (e, i, 0)),
        scratch_shapes=[pltpu.VMEM((tm, D_MODEL), F32)],
        out_shape=jax.ShapeDtypeStruct((E, cap + tm, D_MODEL), BF16),
        compiler_params=pltpu.CompilerParams(
            dimension_semantics=("parallel", "arbitrary"), vmem_limit_bytes=VMEM_LIMIT),
    )(xe, wg, wu, wd)


def _combine_kernel(cnt_s, off_s, ye_hbm, lrt_ref, h_ref, g2_ref, b2_ref, y_ref,
                    ybuf, obuf, sem, osem):
    j = pl.program_id(0)
    nb = pl.num_programs(0)
    E = N_EXPERTS
    RS = SLOT_ROWS
    slot = j % 2

    TR = BF16_SUBLANES

    def fetch(jj, e, s):
        start = off_s[jj * E + e] // TR * TR
        return pltpu.make_async_copy(
            ye_hbm.at[e, pl.ds(pl.multiple_of(start, TR), RS)],
            ybuf.at[s, pl.ds(e * RS, RS)],
            sem.at[s])

    @pl.when(j == 0)
    def _():
        for e in range(E):
            fetch(0, e, 0).start()

    @pl.when(j + 1 < nb)
    def _():
        for e in range(E):
            fetch(j + 1, e, 1 - slot).start()

    for e in range(E):
        fetch(j, e, slot).wait()

    cols = lax.broadcasted_iota(I32, (1, RS), 1)

    def onehot(e, base):
        pos = lrt_ref[:, e:e + 1] + (off_s[j * E + e] % TR)
        return jnp.where(pos == cols + base, 1.0, 0.0).astype(BF16)

    stacked = jnp.concatenate([onehot(e, 0) for e in range(E)], axis=1)
    y_ref[...] = _dot(stacked, ybuf[slot])

    for e in range(E):
        def extra(rc, carry, e=e):
            base = (rc + 1) * RS
            src = pl.multiple_of(off_s[j * E + e] // TR * TR + base, TR)
            cpy = pltpu.make_async_copy(ye_hbm.at[e, pl.ds(src, RS)], obuf, osem.at[0])
            cpy.start()
            cpy.wait()
            y_ref[...] += _dot(onehot(e, base), obuf[...])
            return carry
        chunks = (off_s[j * E + e] % TR + cnt_s[j * E + e] + (RS - 1)) // RS
        lax.fori_loop(0, chunks - 1, extra, 0)

    y_ref[...] = _layer_norm(ALPHA * h_ref[...] + y_ref[...], g2_ref[...], b2_ref[...])


def _combine(cnt_flat, off_flat, ye, lrt, h, g2, b2, tb):
    T = h.shape[0]
    E = N_EXPERTS
    return pl.pallas_call(
        _combine_kernel,
        name="combine",
        grid_spec=pltpu.PrefetchScalarGridSpec(
            num_scalar_prefetch=2,
            grid=(T // tb,),
            in_specs=[
                pl.BlockSpec(memory_space=pl.ANY),
                pl.BlockSpec((tb, E), lambda j, c, o: (j, 0)),
                pl.BlockSpec((tb, D_MODEL), lambda j, c, o: (j, 0)),
                pl.BlockSpec((1, D_MODEL), lambda j, c, o: (0, 0)),
                pl.BlockSpec((1, D_MODEL), lambda j, c, o: (0, 0)),
            ],
            out_specs=pl.BlockSpec((tb, D_MODEL), lambda j, c, o: (j, 0)),
            scratch_shapes=[
                pltpu.VMEM((2, E * SLOT_ROWS, D_MODEL), BF16),
                pltpu.VMEM((SLOT_ROWS, D_MODEL), BF16),
                pltpu.SemaphoreType.DMA((2,)),
                pltpu.SemaphoreType.DMA((1,)),
            ],
        ),
        out_shape=jax.ShapeDtypeStruct((T, D_MODEL), F32),
        compiler_params=pltpu.CompilerParams(
            dimension_semantics=("arbitrary",), vmem_limit_bytes=VMEM_LIMIT),
    )(cnt_flat, off_flat, ye, lrt, h, g2, b2)


def _rope_tables(seq_len):
    half = ROT_DIM // 2
    inv_freq = np.float32(ROPE_THETA) ** (-np.arange(0, ROT_DIM, 2, dtype=np.float32) / ROT_DIM)
    ang = np.arange(seq_len, dtype=np.float32)[:, None] * inv_freq[None, :].astype(np.float32)
    cos = np.cos(ang.astype(np.float64)).astype(np.float32)
    sin = np.sin(ang.astype(np.float64)).astype(np.float32)
    ones = np.ones((seq_len, DIFF_DH - ROT_DIM), np.float32)
    zeros = np.zeros((seq_len, DIFF_DH - ROT_DIM), np.float32)
    zh = np.zeros((seq_len, half), np.float32)
    ra = np.concatenate([cos, cos, ones], axis=1)
    rb = np.concatenate([zh, sin, zeros], axis=1)
    rc = np.concatenate([-sin, zh, zeros], axis=1)
    two = lambda t: jnp.asarray(np.concatenate([t, t], axis=1))
    return two(ra), two(rb), two(rc)


def _moe_tiles(T):
    tb = RANK_SUB
    cap = (CAPACITY_FACTOR * T) // N_EXPERTS
    assert cap % FFN_TILE == 0 and FFN_TILE >= SLOT_ROWS
    return tb, cap, cap + SLOT_ROWS


def _attn_tiles(S):
    return min(1024, S), min(1024, S), min(512, S)


def _values_transposed(dv, B, S):
    vt = dv.reshape(B, S, DIFF_HEADS, DIFF_DV).transpose(0, 2, 3, 1)
    ones = jnp.ones((B, DIFF_HEADS, VT_ROWS - DIFF_DV, S), BF16)
    return jnp.concatenate([vt, ones], axis=2).reshape(B, DIFF_HEADS * VT_ROWS, S)


def _encoder_group(x, wts):
    B, S, D = x.shape
    T = B * S
    x2 = x.reshape(T, D)
    ra, rb, rc = _rope_tables(S)
    qkv, gg, la, dq, dk, dv = _inproj(x2, wts["w_all"], wts["wlr"], wts["blr"], ra, rb, rc, S)
    o_f, o_b = _gla(qkv.reshape(B, S, 1024), la.reshape(B, S, 2 * GLA_QK))
    qb, kb, qc = _attn_tiles(S)
    md = _attn(wts["lam4"], wts["diff_g"], dq.reshape(B, S, DIFF_WIDTH),
               dk.reshape(B, S, DIFF_WIDTH), _values_transposed(dv, B, S), qb, kb, qc)
    h, hb, g3, aff_blocks = _post(
        o_f.reshape(T, GLA_WIDTH), o_b.reshape(T, GLA_WIDTH), gg, md.reshape(T, DIFF_WIDTH), x2,
        wts["gla_g"], wts["w_o"], wts["ln1_g"], wts["ln1_b"], wts["w_router"], wts["w_router_t"])
    tb, cap, cp = _moe_tiles(T)
    lr, cnt, off = _route(aff_blocks, cap, tb)
    cnt_flat = cnt[:, :, 0].reshape(-1)
    off_flat = off[:, :, 0].reshape(-1)
    lrt = lr.transpose(0, 2, 1).reshape(T, N_EXPERTS)
    xe = _gather(cnt_flat, off_flat, hb, g3, lr, cp, tb)
    ye = _ffn(xe, wts["w_gate"], wts["w_up"], wts["w_down"], cap)
    y = _combine(cnt_flat, off_flat, ye, lrt, h, wts["ln2_g"], wts["ln2_b"], tb)
    return y.reshape(B, S, D)


def _prep_weights(w_in, w_lr_f, b_lr_f, w_lr_b, b_lr_b, gla_norm_g,
                  lambda_q1, lambda_k1, lambda_q2, lambda_k2, diff_norm_g, w_o,
                  ln1_g, ln1_b, w_router, w_gate, w_up, w_down, ln2_g, ln2_b):
    sizes = (GLA_QK, GLA_QK, GLA_WIDTH, GLA_WIDTH, GLA_RANK, GLA_RANK,
             DIFF_WIDTH, DIFF_WIDTH, DIFF_WIDTH)
    cols = []
    start = 0
    for s in sizes:
        cols.append(w_in[0][:, start:start + s])
        start += s
    gq, gk, gv, gg, zf, zb, dq, dk, dv = cols
    zpad = jnp.zeros((D_MODEL, LANES - 2 * GLA_RANK), F32)
    w_all = jnp.concatenate([gq, gk, gv, gg, zf, zb, zpad, dq, dk, dv], axis=1).astype(BF16)
    wlr = jnp.zeros((LANES, 2 * GLA_QK), F32)
    wlr = wlr.at[0:GLA_RANK, 0:GLA_QK].set(w_lr_f[0])
    wlr = wlr.at[GLA_RANK:2 * GLA_RANK, GLA_QK:].set(w_lr_b[0])
    return dict(
        w_all=w_all,
        wlr=wlr.astype(BF16),
        blr=jnp.concatenate([b_lr_f[0], b_lr_b[0]])[None, :],
        gla_g=gla_norm_g[0][None, :],
        lam4=jnp.stack([lambda_q1[0], lambda_k1[0], lambda_q2[0], lambda_k2[0]]),
        diff_g=diff_norm_g[0][None, :],
        w_o=w_o[0].astype(BF16),
        ln1_g=ln1_g[0][None, :], ln1_b=ln1_b[0][None, :],
        w_router=w_router[0], w_router_t=w_router[0].T,
        w_gate=w_gate[0].astype(BF16), w_up=w_up[0].astype(BF16), w_down=w_down[0].astype(BF16),
        ln2_g=ln2_g[0][None, :], ln2_b=ln2_b[0][None, :],
    )


def kernel(x_prompt, x_sample, w_in, w_lr_f, b_lr_f, w_lr_b, b_lr_b, gla_norm_g,
           lambda_q1, lambda_k1, lambda_q2, lambda_k2, diff_norm_g, w_o,
           ln1_g, ln1_b, w_router, w_gate, w_up, w_down, ln2_g, ln2_b):
    wts = _prep_weights(w_in, w_lr_f, b_lr_f, w_lr_b, b_lr_b, gla_norm_g,
                        lambda_q1, lambda_k1, lambda_q2, lambda_k2, diff_norm_g, w_o,
                        ln1_g, ln1_b, w_router, w_gate, w_up, w_down, ln2_g, ln2_b)
    return (_encoder_group(x_prompt, wts), _encoder_group(x_sample, wts))
```

```python
import functools
import math

import jax
import jax.numpy as jnp
import numpy as np
from jax import lax
from jax.experimental import pallas as pl
from jax.experimental.pallas import tpu as pltpu

F32 = jnp.float32
BF16 = jnp.bfloat16
I32 = jnp.int32

D_MODEL = 1024
GLA_HEADS = 4
GLA_DK = 64
GLA_DV = 128
GLA_RANK = 16
GLA_GATE_TAU = 16.0
GLA_CHUNK = 64
GLA_QK = GLA_HEADS * GLA_DK
GLA_WIDTH = GLA_HEADS * GLA_DV
DIFF_HEADS = 4
DIFF_DV = 128
DIFF_DH = 64
DIFF_WIDTH = DIFF_HEADS * DIFF_DV
ROT_DIM = DIFF_DH // 4
ROPE_THETA = 500000.0
N_EXPERTS = 16
CAPACITY_FACTOR = 2
EXPERT_FF = 2816
DEPTH = 1
ALPHA = (2 * DEPTH) ** 0.25
EPS = 1e-5
LAMBDA_INIT = 0.8 - 0.6 * math.exp(-0.3 * 0)

LANES = 128
BF16_SUBLANES = 16
VMEM_LIMIT = 56 * 1024 * 1024

ROW_TILE = 512
RANK_SUB = 512
GLA_BLOCK = 512
SLOT_ROWS = 112
XE_WIDTH = D_MODEL + LANES
NOT_SELECTED = -(1 << 20)
FFN_TILE = 512
FFN_FCHUNK = 256
VT_ROWS = DIFF_DV + BF16_SUBLANES
LOG2E = math.log2(math.e)

C_QKV = 0
C_GG = 1024
C_Z = 1536
C_DQK = 1664
C_DV = 2688
W_IN_COLS = 3200


def _nt_dot(a, b):
    return lax.dot_general(a, b, (((1,), (1,)), ((), ())), preferred_element_type=F32)


def _tn_dot(a, b):
    return lax.dot_general(a, b, (((0,), (0,)), ((), ())), preferred_element_type=F32)


def _dot(a, b):
    return jnp.dot(a, b, preferred_element_type=F32)


def _split3(x):
    hi = x.astype(BF16)
    r1 = x - hi.astype(F32)
    md = r1.astype(BF16)
    lo = (r1 - md.astype(F32)).astype(BF16)
    return hi, md, lo


def _layer_norm(v, g, b):
    mu = jnp.mean(v, axis=-1, keepdims=True)
    c = v - mu
    var = jnp.mean(c * c, axis=-1, keepdims=True)
    return c * lax.rsqrt(var + EPS) * g + b


def _silu(v):
    return v / (1.0 + jnp.exp(-v))


def _inproj_kernel(x_ref, w_ref, wlr_ref, blr_ref, ra_ref, rb_ref, rc_ref,
                   qkv_ref, gg_ref, la_ref, dq_ref, dk_ref, vt_ref):
    xb = x_ref[...].astype(BF16)

    def mm(lo, width):
        return _dot(xb, w_ref[:, lo:lo + width])

    g = mm(C_QKV, 1024)
    qkv_ref[:, 0:GLA_QK] = g[:, 0:GLA_QK] * (GLA_DK ** -0.5)
    qkv_ref[:, GLA_QK:1024] = g[:, GLA_QK:1024]
    gg_ref[...] = mm(C_GG, GLA_WIDTH)

    z = mm(C_Z, LANES).astype(BF16)
    pre = _dot(z, wlr_ref[...]) + blr_ref[...]
    log_sig = jnp.minimum(pre, 0.0) - jnp.log(1.0 + jnp.exp(-jnp.abs(pre)))
    la_ref[...] = log_sig * (1.0 / GLA_GATE_TAU)

    d = mm(C_DQK, 2 * DIFF_WIDTH)
    ra = ra_ref[...]
    rb = rb_ref[...]
    rc = rc_ref[...]
    for hh in range(2 * DIFF_HEADS):
        t = d[:, hh * LANES:(hh + 1) * LANES]
        y = t * ra + pltpu.roll(t, ROT_DIM // 2, 1) * rb + pltpu.roll(t, LANES - ROT_DIM // 2, 1) * rc
        if hh < DIFF_HEADS:
            dq_ref[:, hh * LANES:(hh + 1) * LANES] = (y * (LOG2E * DIFF_DH ** -0.5)).astype(BF16)
        else:
            h2 = hh - DIFF_HEADS
            dk_ref[:, h2 * LANES:(h2 + 1) * LANES] = y.astype(BF16)
    dv = mm(C_DV, DIFF_WIDTH)
    ones = jnp.ones((VT_ROWS - DIFF_DV, dv.shape[0]), BF16)
    for h in range(DIFF_HEADS):
        vt_ref[h * VT_ROWS:h * VT_ROWS + DIFF_DV, :] = dv[:, h * DIFF_DV:(h + 1) * DIFF_DV].T.astype(BF16)
        vt_ref[h * VT_ROWS + DIFF_DV:(h + 1) * VT_ROWS, :] = ones


def _inproj(x2, w_all, wlr, blr, ra, rb, rc, seq_len):
    T = x2.shape[0]
    tm = ROW_TILE
    sblocks = seq_len // tm
    row = lambda i: (i, 0)
    const = lambda i: (0, 0)
    rope = lambda i: (i % sblocks, 0)
    vt_map = lambda i: (i // sblocks, 0, i % sblocks)
    return pl.pallas_call(
        _inproj_kernel,
        name="inproj",
        grid=(T // tm,),
        in_specs=[
            pl.BlockSpec((tm, D_MODEL), row),
            pl.BlockSpec((D_MODEL, W_IN_COLS), const),
            pl.BlockSpec((LANES, 2 * GLA_QK), const),
            pl.BlockSpec((1, 2 * GLA_QK), const),
            pl.BlockSpec((tm, LANES), rope),
            pl.BlockSpec((tm, LANES), rope),
            pl.BlockSpec((tm, LANES), rope),
        ],
        out_specs=[
            pl.BlockSpec((tm, 1024), row),
            pl.BlockSpec((tm, GLA_WIDTH), row),
            pl.BlockSpec((tm, 2 * GLA_QK), row),
            pl.BlockSpec((tm, DIFF_WIDTH), row),
            pl.BlockSpec((tm, DIFF_WIDTH), row),
            pl.BlockSpec((None, DIFF_HEADS * VT_ROWS, tm), vt_map),
        ],
        out_shape=[
            jax.ShapeDtypeStruct((T, 1024), F32),
            jax.ShapeDtypeStruct((T, GLA_WIDTH), F32),
            jax.ShapeDtypeStruct((T, 2 * GLA_QK), F32),
            jax.ShapeDtypeStruct((T, DIFF_WIDTH), BF16),
            jax.ShapeDtypeStruct((T, DIFF_WIDTH), BF16),
            jax.ShapeDtypeStruct((T // seq_len, DIFF_HEADS * VT_ROWS, seq_len), BF16),
        ],
        compiler_params=pltpu.CompilerParams(
            dimension_semantics=("parallel",), vmem_limit_bytes=VMEM_LIMIT),
    )(x2, w_all, wlr, blr, ra, rb, rc)


def _gla_kernel(qkvf_ref, laf_ref, qkvb_ref, lab_ref, of_ref, ob_ref, sf_ref, sb_ref, *, nsub):
    @pl.when(pl.program_id(1) == 0)
    def _():
        sf_ref[...] = jnp.zeros(sf_ref.shape, F32)
        sb_ref[...] = jnp.zeros(sb_ref.shape, F32)

    L = GLA_CHUNK
    row = lax.broadcasted_iota(I32, (L, L), 0)
    col = lax.broadcasted_iota(I32, (L, L), 1)
    lower = row >= col
    upper = row <= col
    lane = lax.broadcasted_iota(I32, (1, GLA_QK), 1)
    hmask = [(lane >= h * GLA_DK) & (lane < (h + 1) * GLA_DK) for h in range(GLA_HEADS)]

    def chunk(qkv_ref, la_ref, o_ref, s_ref, c, keep, mid_row, end_row):
        r0 = c * L
        la = la_ref[r0:r0 + L, :]
        tri = jnp.where(keep, 1.0, 0.0).astype(BF16)
        hi, md, lo = _split3(la)
        b = _dot(tri, hi) + _dot(tri, md) + _dot(tri, lo)
        yield
        q = qkv_ref[r0:r0 + L, 0:GLA_QK]
        k = qkv_ref[r0:r0 + L, GLA_QK:2 * GLA_QK]
        bm = b[mid_row:mid_row + 1, :]
        be = b[end_row:end_row + 1, :]
        qs = q * jnp.exp(b - bm)
        ks = (k * jnp.exp(bm - b)).astype(BF16)
        kd = (k * jnp.exp(be - b)).astype(BF16)
        qe = q * jnp.exp(b)
        dec = jnp.exp(be)
        qstack = jnp.concatenate([jnp.where(hmask[h], qs, 0.0) for h in range(GLA_HEADS)],
                                 axis=0).astype(BF16)
        sc = _nt_dot(qstack, ks)
        yield
        vb = qkv_ref[r0:r0 + L, 2 * GLA_QK:1024].astype(BF16)
        for h in range(GLA_HEADS):
            s_h = jnp.where(keep, sc[h * L:(h + 1) * L, :], 0.0).astype(BF16)
            v_h = vb[:, h * GLA_DV:(h + 1) * GLA_DV]
            st = s_ref[h]
            qe_h = jnp.where(hmask[h], qe, 0.0).astype(BF16)
            o_ref[r0:r0 + L, h * GLA_DV:(h + 1) * GLA_DV] = (
                _dot(s_h, v_h) + _nt_dot(qe_h, st.astype(BF16)))
            s_ref[h] = st * dec + _tn_dot(v_h, kd)
        yield

    items = []
    for c in range(nsub):
        items.append(chunk(qkvf_ref, laf_ref, of_ref, sf_ref, c, lower, L // 2 - 1, L - 1))
        items.append(chunk(qkvb_ref, lab_ref, ob_ref, sb_ref, nsub - 1 - c, upper, L // 2, 0))
    n = len(items)
    for i in range(n + 2):
        for stage in range(3):
            if 0 <= i - stage < n:
                next(items[i - stage])


def _gla(qkv, la):
    B, S, _ = qkv.shape
    gb = GLA_BLOCK
    n = S // gb
    fwd = lambda b, i: (b, i, 0)
    bwd = lambda b, i: (b, n - 1 - i, 0)
    bwd_la = lambda b, i: (b, n - 1 - i, 1)
    return pl.pallas_call(
        functools.partial(_gla_kernel, nsub=gb // GLA_CHUNK),
        name="gla",
        grid=(B, n),
        in_specs=[
            pl.BlockSpec((None, gb, 1024), fwd),
            pl.BlockSpec((None, gb, GLA_QK), fwd),
            pl.BlockSpec((None, gb, 1024), bwd),
            pl.BlockSpec((None, gb, GLA_QK), bwd_la),
        ],
        out_specs=[
            pl.BlockSpec((None, gb, GLA_WIDTH), fwd),
            pl.BlockSpec((None, gb, GLA_WIDTH), bwd),
        ],
        out_shape=[jax.ShapeDtypeStruct((B, S, GLA_WIDTH), F32)] * 2,
        scratch_shapes=[pltpu.VMEM((GLA_HEADS, GLA_DV, GLA_QK), F32)] * 2,
        compiler_params=pltpu.CompilerParams(
            dimension_semantics=("parallel", "arbitrary"), vmem_limit_bytes=VMEM_LIMIT),
    )(qkv, la, qkv, la)


def _attn_kernel(lam_ref, g_ref, q_ref, k_ref, vt_ref, o_ref,
                 q1_ref, q2_ref, m1_ref, a1_ref, m2_ref, a2_ref, *, qc):
    ki = pl.program_id(3)
    qb = q_ref.shape[0]

    @pl.when(ki == 0)
    def _():
        q = q_ref[...]
        lane = lax.broadcasted_iota(I32, (1, LANES), 1)
        zero = jnp.zeros_like(q)
        q1_ref[...] = jnp.where(lane < DIFF_DH, q, zero)
        q2_ref[...] = jnp.where(lane >= DIFF_DH, q, zero)
        for m_ref, a_ref in ((m1_ref, a1_ref), (m2_ref, a2_ref)):
            m_ref[...] = jnp.full(m_ref.shape, -jnp.inf, F32)
            a_ref[...] = jnp.zeros(a_ref.shape, F32)

    k = k_ref[...]
    vt = vt_ref[...]
    units = [(c, refs) for c in range(qb // qc)
             for refs in ((q1_ref, m1_ref, a1_ref), (q2_ref, m2_ref, a2_ref))]
    live = {}

    def scores(i):
        c, (qm_ref, m_ref, _) = units[i]
        cs = slice(c * qc, (c + 1) * qc)
        s = _nt_dot(k, qm_ref[cs, :])
        m_old = m_ref[:, cs]
        m_new = jnp.maximum(m_old, jnp.max(s, axis=0, keepdims=True))
        m_ref[:, cs] = m_new
        live[i] = (s, m_new, jnp.exp2(m_old - m_new))

    def probs(i):
        s, m_new, corr = live[i]
        live[i] = (jnp.exp2(s - m_new).astype(BF16), corr)

    def values(i):
        c, (_, _, a_ref) = units[i]
        cs = slice(c * qc, (c + 1) * qc)
        p, corr = live.pop(i)
        a_ref[:, cs] = corr * a_ref[:, cs] + _dot(vt, p)

    n = len(units)
    for i in range(n + 2):
        if i < n:
            scores(i)
        if 0 <= i - 1 < n:
            probs(i - 1)
        if 0 <= i - 2 < n:
            values(i - 2)

    @pl.when(ki == pl.num_programs(3) - 1)
    def _():
        lp = lam_ref[...]
        lam = (jnp.exp(jnp.sum(lp[0:1] * lp[1:2], axis=1, keepdims=True))
               - jnp.exp(jnp.sum(lp[2:3] * lp[3:4], axis=1, keepdims=True)) + LAMBDA_INIT)
        o = (a1_ref[0:DIFF_DV, :] / a1_ref[DIFF_DV:DIFF_DV + 1, :]
             - lam * (a2_ref[0:DIFF_DV, :] / a2_ref[DIFF_DV:DIFF_DV + 1, :]))
        ot = o.T
        r = ot * lax.rsqrt(jnp.mean(ot * ot, axis=1, keepdims=True) + EPS) * g_ref[...]
        o_ref[...] = (r * (1.0 - LAMBDA_INIT)).astype(BF16)


def _attn(lam4, g, dq, dk, dvt, qb, kb, qc):
    B, S, _ = dq.shape
    qmap = lambda b, h, qi, ki: (b, qi, h)
    kmap = lambda b, h, qi, ki: (b, ki, h)
    vmap = lambda b, h, qi, ki: (b, h, ki)
    const = lambda b, h, qi, ki: (0, 0)
    return pl.pallas_call(
        functools.partial(_attn_kernel, qc=qc),
        name="attn",
        grid=(B, DIFF_HEADS, S // qb, S // kb),
        in_specs=[
            pl.BlockSpec((4, DIFF_DH), const),
            pl.BlockSpec((1, DIFF_DV), const),
            pl.BlockSpec((None, qb, LANES), qmap),
            pl.BlockSpec((None, kb, LANES), kmap),
            pl.BlockSpec((None, VT_ROWS, kb), vmap),
        ],
        out_specs=pl.BlockSpec((None, qb, LANES), qmap),
        out_shape=jax.ShapeDtypeStruct((B, S, DIFF_WIDTH), BF16),
        scratch_shapes=[
            pltpu.VMEM((qb, LANES), BF16), pltpu.VMEM((qb, LANES), BF16),
            pltpu.VMEM((1, qb), F32), pltpu.VMEM((VT_ROWS, qb), F32),
            pltpu.VMEM((1, qb), F32), pltpu.VMEM((VT_ROWS, qb), F32),
        ],
        compiler_params=pltpu.CompilerParams(
            dimension_semantics=("parallel", "parallel", "parallel", "arbitrary"),
            vmem_limit_bytes=VMEM_LIMIT),
    )(lam4, g, dq, dk, dvt)


def _post_kernel(of_ref, ob_ref, gg_ref, md_ref, x_ref, gn_ref, wo_ref, g1_ref, b1_ref,
                 wrt_ref, h_ref, hb_ref, g3_ref, afft_ref):
    gn = gn_ref[...]
    parts = []
    for h in range(GLA_HEADS):
        sl = slice(h * GLA_DV, (h + 1) * GLA_DV)
        o = of_ref[:, sl] + ob_ref[:, sl]
        r = o * lax.rsqrt(jnp.mean(o * o, axis=1, keepdims=True) + EPS) * gn
        parts.append((r * _silu(gg_ref[:, sl])).astype(BF16))
    mixed_gla = jnp.concatenate(parts, axis=1)
    mix = _dot(mixed_gla, wo_ref[0:GLA_WIDTH, :]) + _dot(md_ref[...], wo_ref[GLA_WIDTH:, :])
    hval = _layer_norm(ALPHA * x_ref[...] + mix, g1_ref[...], b1_ref[...])
    h_ref[...] = hval
    hh = hval.astype(BF16)
    hb_ref[...] = hh
    hl = (hval - hh.astype(F32)).astype(BF16)
    wrt = wrt_ref[...]
    wth = wrt.astype(BF16)
    wtl = (wrt - wth.astype(F32)).astype(BF16)
    lt = _nt_dot(wth, hh) + _nt_dot(wth, hl) + _nt_dot(wtl, hh)
    et = jnp.exp(lt - jnp.max(lt, axis=0, keepdims=True))
    afft = et / jnp.sum(et, axis=0, keepdims=True)
    afft_ref[...] = afft
    a_hi, a_md, a_lo = _split3(afft)
    pad = jnp.zeros((LANES - 3 * N_EXPERTS, afft.shape[1]), F32)
    g3t = jnp.concatenate([a_hi.astype(F32), a_md.astype(F32), a_lo.astype(F32), pad], axis=0)
    g3_ref[...] = g3t.T.astype(BF16)


def _post(o_f, o_b, gg, md, x2, gn, wo, g1, b1, wrt):
    T = x2.shape[0]
    tm = ROW_TILE
    assert tm == RANK_SUB
    row = lambda i: (i, 0)
    const = lambda i: (0, 0)
    return pl.pallas_call(
        _post_kernel,
        name="post",
        grid=(T // tm,),
        in_specs=[
            pl.BlockSpec((tm, GLA_WIDTH), row),
            pl.BlockSpec((tm, GLA_WIDTH), row),
            pl.BlockSpec((tm, GLA_WIDTH), row),
            pl.BlockSpec((tm, DIFF_WIDTH), row),
            pl.BlockSpec((tm, D_MODEL), row),
            pl.BlockSpec((1, GLA_DV), const),
            pl.BlockSpec((D_MODEL, D_MODEL), const),
            pl.BlockSpec((1, D_MODEL), const),
            pl.BlockSpec((1, D_MODEL), const),
            pl.BlockSpec((N_EXPERTS, D_MODEL), const),
        ],
        out_specs=[
            pl.BlockSpec((tm, D_MODEL), row),
            pl.BlockSpec((tm, D_MODEL), row),
            pl.BlockSpec((tm, LANES), row),
            pl.BlockSpec((None, N_EXPERTS, tm), lambda i: (i, 0, 0)),
        ],
        out_shape=[
            jax.ShapeDtypeStruct((T, D_MODEL), F32),
            jax.ShapeDtypeStruct((T, D_MODEL), BF16),
            jax.ShapeDtypeStruct((T, LANES), BF16),
            jax.ShapeDtypeStruct((T // tm, N_EXPERTS, tm), F32),
        ],
        compiler_params=pltpu.CompilerParams(
            dimension_semantics=("parallel",), vmem_limit_bytes=VMEM_LIMIT),
    )(o_f, o_b, gg, md, x2, gn, wo, g1, b1, wrt)


def _route_kernel(aff_ref, lr_ref, cnt_ref, off_ref, *, cap, nblk, spb):
    E = N_EXPERTS

    def count(mask):
        part = jnp.sum(jnp.where(mask, 1, 0).astype(I32), axis=0)
        return jnp.sum(part, axis=1, keepdims=True)

    def search(i, thr_bits):
        cand = thr_bits | jnp.left_shift(jnp.int32(1), 30 - i)
        cand_f = pltpu.bitcast(cand, F32)
        return jnp.where(count(aff_ref[...] >= cand_f[None]) >= cap, cand, thr_bits)

    thr = pltpu.bitcast(lax.fori_loop(0, 31, search, jnp.zeros((E, 1), I32)), F32)
    need = cap - count(aff_ref[...] > thr[None])

    r = lax.broadcasted_iota(I32, (RANK_SUB, RANK_SUB), 0)
    c = lax.broadcasted_iota(I32, (RANK_SUB, RANK_SUB), 1)
    before = jnp.where(r < c, 1.0, 0.0).astype(BF16)

    def block(j, carry):
        eqc, offc = carry
        selc = jnp.zeros((E, 1), I32)
        for s in range(spb):
            idx = j * spb + s
            a = aff_ref[idx]
            gt = a > thr
            eq = a == thr
            eqrank = _dot(jnp.where(eq, 1.0, 0.0).astype(BF16), before).astype(I32) + eqc
            sel = gt | (eq & (eqrank < need))
            rank = _dot(jnp.where(sel, 1.0, 0.0).astype(BF16), before).astype(I32) + selc
            lr_ref[idx] = jnp.where(sel, rank, NOT_SELECTED)
            eqc = eqc + jnp.sum(jnp.where(eq, 1, 0).astype(I32), axis=1, keepdims=True)
            selc = selc + jnp.sum(jnp.where(sel, 1, 0).astype(I32), axis=1, keepdims=True)
        cnt_ref[j] = jnp.broadcast_to(selc, (E, LANES))
        off_ref[j] = jnp.broadcast_to(offc, (E, LANES))
        return eqc, offc + selc

    zero = jnp.zeros((E, 1), I32)
    lax.fori_loop(0, nblk, block, (zero, zero))


def _route(aff_blocks, cap, tb):
    nsb = aff_blocks.shape[0]
    spb = tb // RANK_SUB
    nblk = nsb // spb
    E = N_EXPERTS
    return pl.pallas_call(
        functools.partial(_route_kernel, cap=cap, nblk=nblk, spb=spb),
        name="route",
        out_shape=[
            jax.ShapeDtypeStruct((nsb, E, RANK_SUB), I32),
            jax.ShapeDtypeStruct((nblk, E, LANES), I32),
            jax.ShapeDtypeStruct((nblk, E, LANES), I32),
        ],
        compiler_params=pltpu.CompilerParams(vmem_limit_bytes=VMEM_LIMIT),
    )(aff_blocks)


def _gather_kernel(cnt_s, off_s, h_ref, g3_ref, lr_ref, xe_out, haug, xbuf, obuf, zbuf, tail, sem,
                   osem, *, cp):
    j = pl.program_id(0)
    nb = pl.num_programs(0)
    E = N_EXPERTS
    RS = SLOT_ROWS
    TR = BF16_SUBLANES
    slot = j % 2

    @pl.when(j == 0)
    def _():
        tail[...] = jnp.zeros(tail.shape, BF16)

    haug[:, 0:D_MODEL] = h_ref[...]
    haug[:, D_MODEL:XE_WIDTH] = g3_ref[...]
    rows = lax.broadcasted_iota(I32, (RS, 1), 0)

    def fill(jj, e):
        return off_s[jj * E + e]

    def onehot(e, base):
        pos = lr_ref[0, e:e + 1, :] + (fill(j, e) % TR)
        return jnp.where(pos == rows + base, 1.0, 0.0).astype(BF16)

    half = E // 2
    for g in range(2):
        stacked = jnp.concatenate([onehot(e, 0) for e in range(g * half, (g + 1) * half)], axis=0)
        xbuf[slot, g * half * RS:(g + 1) * half * RS, :] = _dot(stacked, haug[...]).astype(BF16)

    def copy(jj, e, s):
        start = fill(jj, e) // TR * TR
        return pltpu.make_async_copy(
            xbuf.at[s, pl.ds(e * RS, RS)],
            xe_out.at[e, pl.ds(pl.multiple_of(start, TR), RS)],
            sem.at[s])

    def chunks(e):
        return (fill(j, e) % TR + cnt_s[j * E + e] + (RS - 1)) // RS

    for e in range(E):
        r0 = e * RS
        xbuf[slot, r0:r0 + TR, :] = (xbuf[slot, r0:r0 + TR, :].astype(F32)
                                      + tail[e].astype(F32)).astype(BF16)
        end = fill(j, e) % TR + cnt_s[j * E + e]
        keep_at = end // TR * TR
        in_first = jnp.logical_and(keep_at < RS, end % TR > 0)
        src = pl.multiple_of(jnp.minimum(keep_at, RS - TR), TR)
        part = xbuf[slot, pl.ds(r0 + src, TR), :]
        tail[e] = jnp.where(in_first, part, jnp.zeros_like(part))

    @pl.when(j > 0)
    def _():
        for e in range(E):
            copy(j - 1, e, 1 - slot).wait()

    for e in range(E):
        copy(j, e, slot).start()

    for e in range(E):
        def extra(rc, carry, e=e):
            base = (rc + 1) * RS
            obuf[...] = _dot(onehot(e, base), haug[...]).astype(BF16)
            end = fill(j, e) % TR + cnt_s[j * E + e]
            keep_at = end // TR * TR

            @pl.when(jnp.logical_and(keep_at // RS == rc + 1, end % TR > 0))
            def _():
                tail[e] = obuf[pl.ds(pl.multiple_of(keep_at - base, TR), TR), :]

            dst = pl.multiple_of(fill(j, e) // TR * TR + base, TR)
            cpy = pltpu.make_async_copy(obuf, xe_out.at[e, pl.ds(dst, RS)], osem.at[0])
            cpy.start()
            cpy.wait()
            return carry
        lax.fori_loop(0, chunks(e) - 1, extra, 0)

    @pl.when(j == nb - 1)
    def _():
        for e in range(E):
            copy(j, e, slot).wait()
        zbuf[...] = jnp.zeros(zbuf.shape, BF16)
        ZR = BF16_SUBLANES

        def zcopy(e, z0, i):
            return pltpu.make_async_copy(
                zbuf, xe_out.at[e, pl.ds(pl.multiple_of(z0 + i * ZR, ZR), ZR)], osem.at[0])

        for e in range(E):
            z0 = fill(j, e) // TR * TR + jnp.maximum(chunks(e), 1) * RS
            n = (cp - z0) // ZR

            def zs(i, carry, e=e, z0=z0):
                zcopy(e, z0, i).start()
                return carry

            def zw(i, carry, e=e, z0=z0):
                zcopy(e, z0, i).wait()
                return carry
            lax.fori_loop(0, n, zs, 0)
            lax.fori_loop(0, n, zw, 0)


def _gather(cnt_flat, off_flat, hb, g3, lr, cp, tb):
    T = hb.shape[0]
    E = N_EXPERTS
    assert tb == RANK_SUB
    return pl.pallas_call(
        functools.partial(_gather_kernel, cp=cp),
        name="gather",
        grid_spec=pltpu.PrefetchScalarGridSpec(
            num_scalar_prefetch=2,
            grid=(T // tb,),
            in_specs=[
                pl.BlockSpec((tb, D_MODEL), lambda j, c, o: (j, 0)),
                pl.BlockSpec((tb, LANES), lambda j, c, o: (j, 0)),
                pl.BlockSpec((1, E, tb), lambda j, c, o: (j, 0, 0)),
            ],
            out_specs=pl.BlockSpec(memory_space=pl.ANY),
            scratch_shapes=[
                pltpu.VMEM((tb, XE_WIDTH), BF16),
                pltpu.VMEM((2, E * SLOT_ROWS, XE_WIDTH), BF16),
                pltpu.VMEM((SLOT_ROWS, XE_WIDTH), BF16),
                pltpu.VMEM((BF16_SUBLANES, XE_WIDTH), BF16),
                pltpu.VMEM((E, BF16_SUBLANES, XE_WIDTH), BF16),
                pltpu.SemaphoreType.DMA((2,)),
                pltpu.SemaphoreType.DMA((1,)),
            ],
        ),
        out_shape=jax.ShapeDtypeStruct((E, cp, XE_WIDTH), BF16),
        compiler_params=pltpu.CompilerParams(
            dimension_semantics=("arbitrary",), vmem_limit_bytes=VMEM_LIMIT),
    )(cnt_flat, off_flat, hb, g3, lr)


def _ffn_kernel(x_ref, wg_ref, wu_ref, wd_ref, y_ref, acc_ref):
    e = pl.program_id(0)
    i = pl.program_id(1)
    live = i < pl.num_programs(1) - 1

    @pl.when(live)
    def _():
        x = x_ref[:, 0:D_MODEL]
        for c in range(EXPERT_FF // FFN_FCHUNK):
            cs = slice(c * FFN_FCHUNK, (c + 1) * FFN_FCHUNK)
            g = _dot(x, wg_ref[:, cs])
            u = _dot(x, wu_ref[:, cs])
            hid = (_silu(g) * u).astype(BF16)
            part = _dot(hid, wd_ref[cs, :])
            if c == 0:
                acc_ref[...] = part
            else:
                acc_ref[...] += part
        lane = lax.broadcasted_iota(I32, (1, LANES), 1)
        mine = (lane == e) | (lane == e + N_EXPERTS) | (lane == e + 2 * N_EXPERTS)
        pieces = x_ref[:, D_MODEL:XE_WIDTH].astype(F32)
        gate = jnp.sum(jnp.where(mine, pieces, 0.0), axis=1, keepdims=True)
        y_ref[...] = (acc_ref[...] * gate).astype(BF16)

    @pl.when(jnp.logical_not(live))
    def _():
        y_ref[...] = jnp.zeros(y_ref.shape, BF16)


def _ffn(xe, wg, wu, wd, cap):
    E = xe.shape[0]
    tm = FFN_TILE
    nt = cap // tm
    return pl.pallas_call(
        _ffn_kernel,
        name="ffn",
        grid=(E, nt + 1),
        in_specs=[
            pl.BlockSpec((None, tm, XE_WIDTH), lambda e, i: (e, jnp.minimum(i, nt - 1), 0)),
            pl.BlockSpec((None, D_MODEL, EXPERT_FF), lambda e, i: (e, 0, 0)),
            pl.BlockSpec((None, D_MODEL, EXPERT_FF), lambda e, i: (e, 0, 0)),
            pl.BlockSpec((None, EXPERT_FF, D_MODEL), lambda e, i: (e, 0, 0)),
        ],
        out_specs=pl.BlockSpec((None, tm, D_MODEL), lambda e, i: (e, i, 0)),
        scratch_shapes=[pltpu.VMEM((tm, D_MODEL), F32)],
        out_shape=jax.ShapeDtypeStruct((E, cap + tm, D_MODEL), BF16),
        compiler_params=pltpu.CompilerParams(
            dimension_semantics=("parallel", "arbitrary"), vmem_limit_bytes=VMEM_LIMIT),
    )(xe, wg, wu, wd)


def _combine_kernel(cnt_s, off_s, ye_hbm, lrt_ref, h_ref, g2_ref, b2_ref, y_ref,
                    ybuf, obuf, sem, osem):
    j = pl.program_id(0)
    nb = pl.num_programs(0)
    E = N_EXPERTS
    RS = SLOT_ROWS
    slot = j % 2

    TR = BF16_SUBLANES

    def fetch(jj, e, s):
        start = off_s[jj * E + e] // TR * TR
        return pltpu.make_async_copy(
            ye_hbm.at[e, pl.ds(pl.multiple_of(start, TR), RS)],
            ybuf.at[s, pl.ds(e * RS, RS)],
            sem.at[s])

    @pl.when(j == 0)
    def _():
        for e in range(E):
            fetch(0, e, 0).start()

    @pl.when(j + 1 < nb)
    def _():
        for e in range(E):
            fetch(j + 1, e, 1 - slot).start()

    for e in range(E):
        fetch(j, e, slot).wait()

    cols = lax.broadcasted_iota(I32, (1, RS), 1)

    def onehot(e, base):
        pos = lrt_ref[:, e:e + 1] + (off_s[j * E + e] % TR)
        return jnp.where(pos == cols + base, 1.0, 0.0).astype(BF16)

    stacked = jnp.concatenate([onehot(e, 0) for e in range(E)], axis=1)
    y_ref[...] = _dot(stacked, ybuf[slot])

    for e in range(E):
        def extra(rc, carry, e=e):
            base = (rc + 1) * RS
            src = pl.multiple_of(off_s[j * E + e] // TR * TR + base, TR)
            cpy = pltpu.make_async_copy(ye_hbm.at[e, pl.ds(src, RS)], obuf, osem.at[0])
            cpy.start()
            cpy.wait()
            y_ref[...] += _dot(onehot(e, base), obuf[...])
            return carry
        chunks = (off_s[j * E + e] % TR + cnt_s[j * E + e] + (RS - 1)) // RS
        lax.fori_loop(0, chunks - 1, extra, 0)

    y_ref[...] = _layer_norm(ALPHA * h_ref[...] + y_ref[...], g2_ref[...], b2_ref[...])


def _combine(cnt_flat, off_flat, ye, lrt, h, g2, b2, tb):
    T = h.shape[0]
    E = N_EXPERTS
    return pl.pallas_call(
        _combine_kernel,
        name="combine",
        grid_spec=pltpu.PrefetchScalarGridSpec(
            num_scalar_prefetch=2,
            grid=(T // tb,),
            in_specs=[
                pl.BlockSpec(memory_space=pl.ANY),
                pl.BlockSpec((tb, E), lambda j, c, o: (j, 0)),
                pl.BlockSpec((tb, D_MODEL), lambda j, c, o: (j, 0)),
                pl.BlockSpec((1, D_MODEL), lambda j, c, o: (0, 0)),
                pl.BlockSpec((1, D_MODEL), lambda j, c, o: (0, 0)),
            ],
            out_specs=pl.BlockSpec((tb, D_MODEL), lambda j, c, o: (j, 0)),
            scratch_shapes=[
                pltpu.VMEM((2, E * SLOT_ROWS, D_MODEL), BF16),
                pltpu.VMEM((SLOT_ROWS, D_MODEL), BF16),
                pltpu.SemaphoreType.DMA((2,)),
                pltpu.SemaphoreType.DMA((1,)),
            ],
        ),
        out_shape=jax.ShapeDtypeStruct((T, D_MODEL), F32),
        compiler_params=pltpu.CompilerParams(
            dimension_semantics=("arbitrary",), vmem_limit_bytes=VMEM_LIMIT),
    )(cnt_flat, off_flat, ye, lrt, h, g2, b2)


def _rope_tables(seq_len):
    half = ROT_DIM // 2
    inv_freq = np.float32(ROPE_THETA) ** (-np.arange(0, ROT_DIM, 2, dtype=np.float32) / ROT_DIM)
    ang = np.arange(seq_len, dtype=np.float32)[:, None] * inv_freq[None, :].astype(np.float32)
    cos = np.cos(ang.astype(np.float64)).astype(np.float32)
    sin = np.sin(ang.astype(np.float64)).astype(np.float32)
    ones = np.ones((seq_len, DIFF_DH - ROT_DIM), np.float32)
    zeros = np.zeros((seq_len, DIFF_DH - ROT_DIM), np.float32)
    zh = np.zeros((seq_len, half), np.float32)
    ra = np.concatenate([cos, cos, ones], axis=1)
    rb = np.concatenate([zh, sin, zeros], axis=1)
    rc = np.concatenate([-sin, zh, zeros], axis=1)
    two = lambda t: jnp.asarray(np.concatenate([t, t], axis=1))
    return two(ra), two(rb), two(rc)


def _moe_tiles(T):
    tb = RANK_SUB
    cap = (CAPACITY_FACTOR * T) // N_EXPERTS
    assert cap % FFN_TILE == 0 and FFN_TILE >= SLOT_ROWS
    return tb, cap, cap + SLOT_ROWS


def _attn_tiles(S):
    return min(2048, S), min(1024, S), min(512, S)


def _encoder_group(x, wts):
    B, S, D = x.shape
    T = B * S
    x2 = x.reshape(T, D)
    ra, rb, rc = _rope_tables(S)
    qkv, gg, la, dq, dk, dvt = _inproj(x2, wts["w_all"], wts["wlr"], wts["blr"], ra, rb, rc, S)
    o_f, o_b = _gla(qkv.reshape(B, S, 1024), la.reshape(B, S, 2 * GLA_QK))
    qb, kb, qc = _attn_tiles(S)
    md = _attn(wts["lam4"], wts["diff_g"], dq.reshape(B, S, DIFF_WIDTH),
               dk.reshape(B, S, DIFF_WIDTH), dvt, qb, kb, qc)
    h, hb, g3, aff_blocks = _post(
        o_f.reshape(T, GLA_WIDTH), o_b.reshape(T, GLA_WIDTH), gg, md.reshape(T, DIFF_WIDTH), x2,
        wts["gla_g"], wts["w_o"], wts["ln1_g"], wts["ln1_b"], wts["w_router_t"])
    tb, cap, cp = _moe_tiles(T)
    lr, cnt, off = _route(aff_blocks, cap, tb)
    cnt_flat = cnt[:, :, 0].reshape(-1)
    off_flat = off[:, :, 0].reshape(-1)
    lrt = lr.transpose(0, 2, 1).reshape(T, N_EXPERTS)
    xe = _gather(cnt_flat, off_flat, hb, g3, lr, cp, tb)
    ye = _ffn(xe, wts["w_gate"], wts["w_up"], wts["w_down"], cap)
    y = _combine(cnt_flat, off_flat, ye, lrt, h, wts["ln2_g"], wts["ln2_b"], tb)
    return y.reshape(B, S, D)


def _prep_weights(w_in, w_lr_f, b_lr_f, w_lr_b, b_lr_b, gla_norm_g,
                  lambda_q1, lambda_k1, lambda_q2, lambda_k2, diff_norm_g, w_o,
                  ln1_g, ln1_b, w_router, w_gate, w_up, w_down, ln2_g, ln2_b):
    sizes = (GLA_QK, GLA_QK, GLA_WIDTH, GLA_WIDTH, GLA_RANK, GLA_RANK,
             DIFF_WIDTH, DIFF_WIDTH, DIFF_WIDTH)
    cols = []
    start = 0
    for s in sizes:
        cols.append(w_in[0][:, start:start + s])
        start += s
    gq, gk, gv, gg, zf, zb, dq, dk, dv = cols
    zpad = jnp.zeros((D_MODEL, LANES - 2 * GLA_RANK), F32)
    w_all = jnp.concatenate([gq, gk, gv, gg, zf, zb, zpad, dq, dk, dv], axis=1).astype(BF16)
    wlr = jnp.zeros((LANES, 2 * GLA_QK), F32)
    wlr = wlr.at[0:GLA_RANK, 0:GLA_QK].set(w_lr_f[0])
    wlr = wlr.at[GLA_RANK:2 * GLA_RANK, GLA_QK:].set(w_lr_b[0])
    return dict(
        w_all=w_all,
        wlr=wlr.astype(BF16),
        blr=jnp.concatenate([b_lr_f[0], b_lr_b[0]])[None, :],
        gla_g=gla_norm_g[0][None, :],
        lam4=jnp.stack([lambda_q1[0], lambda_k1[0], lambda_q2[0], lambda_k2[0]]),
        diff_g=diff_norm_g[0][None, :],
        w_o=w_o[0].astype(BF16),
        ln1_g=ln1_g[0][None, :], ln1_b=ln1_b[0][None, :],
        w_router_t=w_router[0].T,
        w_gate=w_gate[0].astype(BF16), w_up=w_up[0].astype(BF16), w_down=w_down[0].astype(BF16),
        ln2_g=ln2_g[0][None, :], ln2_b=ln2_b[0][None, :],
    )


def kernel(x_prompt, x_sample, w_in, w_lr_f, b_lr_f, w_lr_b, b_lr_b, gla_norm_g,
           lambda_q1, lambda_k1, lambda_q2, lambda_k2, diff_norm_g, w_o,
           ln1_g, ln1_b, w_router, w_gate, w_up, w_down, ln2_g, ln2_b):
    wts = _prep_weights(w_in, w_lr_f, b_lr_f, w_lr_b, b_lr_b, gla_norm_g,
                        lambda_q1, lambda_k1, lambda_q2, lambda_k2, diff_norm_g, w_o,
                        ln1_g, ln1_b, w_router, w_gate, w_up, w_down, ln2_g, ln2_b)
    return (_encoder_group(x_prompt, wts), _encoder_group(x_sample, wts))
```
